```python
import jax, jax.numpy as jnp
from jax import lax
import numpy as np

D_MODEL = 1024
BATCH = 2
SEQ = 8192
DEPTH = 1
DEC_BATCH = 128
DEC_SEQ = 1
PAST_LEN = 8192
PAGE_SIZE = 128

HEAD_DIM = 64
NSA_HEADS = 8
NSA_KV_HEADS = 2
NSA_GROUP = NSA_HEADS // NSA_KV_HEADS
FOX_HEADS = 8
MIX_WIDTH = (NSA_HEADS + FOX_HEADS) * HEAD_DIM
CMP_BLOCK = 32
SLC_BLOCK = 64
SLC_RATIO = SLC_BLOCK // CMP_BLOCK
SLC_TOPK = 16
WINDOW = 512
N_BRANCH = 3
Q_BLOCK = 128
FORCE_BONUS = 1000.0
N_EXPERTS = 64
N_EXPERT_GROUPS = 8
TOPK_GROUPS = 4
GROUP_SCORE_K = 2
TOP_K = 4
D_EXPERT = 256
D_SHARED = 256
ROUTED_SCALE = 2.5
LN_EPS = 1e-5
DN_ALPHA = (2.0 * DEPTH) ** 0.25
DN_BETA = (8.0 * DEPTH) ** -0.25
NEG_INF = -1e30

PROJ_SPLITS = (NSA_HEADS * HEAD_DIM, NSA_KV_HEADS * HEAD_DIM, NSA_KV_HEADS * HEAD_DIM,
               NSA_KV_HEADS * HEAD_DIM, NSA_KV_HEADS * HEAD_DIM, NSA_KV_HEADS * HEAD_DIM,
               NSA_KV_HEADS * HEAD_DIM, NSA_HEADS * N_BRANCH, FOX_HEADS * HEAD_DIM,
               FOX_HEADS * HEAD_DIM, FOX_HEADS * HEAD_DIM, FOX_HEADS)
PROJ_IS_VALUE = (False, False, True, False, True, False, True, False, False, False, True, False)
D_PROJ = NSA_HEADS * HEAD_DIM + 6 * NSA_KV_HEADS * HEAD_DIM + NSA_HEADS * N_BRANCH + 3 * FOX_HEADS * HEAD_DIM + FOX_HEADS

STATE_NAMES = ('p_fox_k', 'p_fox_v', 'p_fox_logf', 'p_cmp_k', 'p_cmp_v', 'p_slc_k', 'p_slc_v', 'p_win_k', 'p_win_v',
               's_fox_k', 's_fox_v', 's_fox_logf', 's_cmp_k', 's_cmp_v', 's_slc_k', 's_slc_v', 's_win_k', 's_win_v')

kernel_name = 'hymba_nsa_fox_moe_deepnorm_step'


def _layer_norm(x, g, b):
    xf = x.astype(jnp.float32)
    mu = xf.mean(-1, keepdims=True)
    var = jnp.square(xf - mu).mean(-1, keepdims=True)
    return ((xf - mu) * lax.rsqrt(var + LN_EPS) * g + b).astype(x.dtype)


def _masked_softmax(logits, mask):
    logits = jnp.where(mask, logits, NEG_INF)
    m = jnp.max(logits, axis=-1, keepdims=True)
    p = jnp.where(mask, jnp.exp(logits - m), 0.0)
    return p / jnp.maximum(jnp.sum(p, axis=-1, keepdims=True), 1e-30)


def _alibi_slopes():
    h = jnp.arange(1, NSA_HEADS + 1, dtype=jnp.float32)
    return (2.0 ** (-8.0 * h / NSA_HEADS)).reshape(NSA_KV_HEADS, NSA_GROUP)


def _project(x, w_in, b_in):
    B, T = x.shape[0], x.shape[1]
    h = jnp.einsum('btd,dn->btn', x, w_in) + b_in
    offs = np.cumsum(PROJ_SPLITS)[:-1].tolist()
    q_n, k_c, v_c, k_s, v_s, k_w, v_w, g_n, q_f, k_f, v_f, f_f = jnp.split(h, offs, axis=-1)
    kv = lambda a: a.reshape(B, T, NSA_KV_HEADS, HEAD_DIM)
    fh = lambda a: a.reshape(B, T, FOX_HEADS, HEAD_DIM)
    return {
        'q_nsa': q_n.reshape(B, T, NSA_KV_HEADS, NSA_GROUP, HEAD_DIM),
        'k_cmp': kv(k_c), 'v_cmp': kv(v_c), 'k_slc': kv(k_s), 'v_slc': kv(v_s),
        'k_win': kv(k_w), 'v_win': kv(v_w),
        'gates': jax.nn.sigmoid(g_n.reshape(B, T, NSA_KV_HEADS, NSA_GROUP, N_BRANCH)),
        'q_fox': fh(q_f), 'k_fox': fh(k_f), 'v_fox': fh(v_f),
        'logf': jax.nn.log_sigmoid(f_f.astype(jnp.float32)),
    }


def _compress(rows, w_c):
    B, L = rows.shape[0], rows.shape[1]
    blocks = rows.reshape(B, L // CMP_BLOCK, CMP_BLOCK, NSA_KV_HEADS, HEAD_DIM)
    return jnp.einsum('bnigd,ide->bnge', blocks, w_c)


def _nsa_attend(q, gates, pos, kc, vc, gather_slc, n_slc, wk, wv, w_pos, slopes):
    B, Tq = q.shape[0], q.shape[1]
    f32 = jnp.float32
    scale = HEAD_DIM ** -0.5
    sl = slopes[:, :, None, None]
    n_c = kc.shape[1]
    c_end = jnp.arange(n_c, dtype=jnp.int32) * CMP_BLOCK + (CMP_BLOCK - 1)
    dist_c = (pos[:, None] - c_end[None, :]).astype(f32)
    lc = jnp.einsum('bqgrd,bcgd->bgrqc', q, kc).astype(f32) * scale - sl * dist_c
    pc = _masked_softmax(lc, dist_c >= 0)
    o_cmp = jnp.einsum('bgrqc,bcgd->bqgrd', pc.astype(vc.dtype), vc)
    imp = jnp.pad(pc.sum(axis=2), ((0, 0), (0, 0), (0, 0), (0, SLC_RATIO * n_slc - n_c)))
    imp = imp.reshape(B, NSA_KV_HEADS, Tq, n_slc, SLC_RATIO).sum(-1)
    blk = jnp.arange(n_slc, dtype=jnp.int32)
    forced = (blk[None, :] == 0) | (blk[None, :] == (pos // SLC_BLOCK)[:, None])
    score = jnp.where(blk[None, :] * SLC_BLOCK <= pos[:, None], imp + FORCE_BONUS * forced.astype(f32), -1.0)
    _, sel = lax.top_k(score, min(SLC_TOPK, n_slc))
    ks, vs = gather_slc(sel)
    kpos = sel[..., None] * SLC_BLOCK + jnp.arange(SLC_BLOCK, dtype=jnp.int32)
    dist_s = (pos[None, None, :, None, None] - kpos).astype(f32)[:, :, None]
    ls = jnp.einsum('bqgrd,bgqksd->bgrqks', q, ks).astype(f32) * scale - slopes[None, :, :, None, None, None] * dist_s
    shp = ls.shape
    ps = _masked_softmax(ls.reshape(shp[:4] + (-1,)), (dist_s >= 0).reshape(shp[0], shp[1], 1, shp[3], -1)).reshape(shp)
    o_slc = jnp.einsum('bgrqks,bgqksd->bqgrd', ps.astype(vs.dtype), vs)
    dist_w = pos[:, None] - w_pos[None, :]
    mask_w = (dist_w >= 0) & (dist_w < WINDOW) & (w_pos[None, :] >= 0)
    lw = jnp.einsum('bqgrd,bkgd->bgrqk', q, wk).astype(f32) * scale - sl * dist_w.astype(f32)
    pw = _masked_softmax(lw, mask_w)
    o_win = jnp.einsum('bgrqk,bkgd->bqgrd', pw.astype(wv.dtype), wv)
    return gates[..., 0:1] * o_cmp + gates[..., 1:2] * o_slc + gates[..., 2:3] * o_win


def _nsa_prompt(p, w_cmp_k, w_cmp_v, slopes):
    q = p['q_nsa']
    B, T = q.shape[0], q.shape[1]
    n_c = T // CMP_BLOCK
    kc = _compress(p['k_cmp'][:, :n_c * CMP_BLOCK], w_cmp_k)
    vc = _compress(p['v_cmp'][:, :n_c * CMP_BLOCK], w_cmp_v)
    n_s = -(-T // SLC_BLOCK)
    pad_s = ((0, 0), (0, n_s * SLC_BLOCK - T), (0, 0), (0, 0))
    ks_b = jnp.pad(p['k_slc'], pad_s).reshape(B, n_s, SLC_BLOCK, NSA_KV_HEADS, HEAD_DIM)
    vs_b = jnp.pad(p['v_slc'], pad_s).reshape(B, n_s, SLC_BLOCK, NSA_KV_HEADS, HEAD_DIM)
    b_idx = jnp.arange(B)[:, None, None, None]
    g_idx = jnp.arange(NSA_KV_HEADS)[None, :, None, None]

    def gather_slc(sel):
        return ks_b[b_idx, sel, :, g_idx], vs_b[b_idx, sel, :, g_idx]

    pad_w = ((0, 0), (WINDOW, 0), (0, 0), (0, 0))
    kw = jnp.pad(p['k_win'], pad_w)
    vw = jnp.pad(p['v_win'], pad_w)

    def block(i):
        q0 = i * Q_BLOCK
        pos = q0 + jnp.arange(Q_BLOCK, dtype=jnp.int32)
        w_pos = q0 - WINDOW + jnp.arange(WINDOW + Q_BLOCK, dtype=jnp.int32)
        cut = lambda a, n: lax.dynamic_slice_in_dim(a, q0, n, axis=1)
        return _nsa_attend(cut(q, Q_BLOCK), cut(p['gates'], Q_BLOCK), pos, kc, vc, gather_slc, n_s,
                           cut(kw, WINDOW + Q_BLOCK), cut(vw, WINDOW + Q_BLOCK), w_pos, slopes)

    o = lax.map(block, jnp.arange(T // Q_BLOCK, dtype=jnp.int32))
    return jnp.moveaxis(o, 0, 1).reshape(B, T, NSA_HEADS * HEAD_DIM)


def _nsa_sample(p, l, cache_cmp_k, cache_cmp_v, cache_slc_k, cache_slc_v, win_k_buf, win_v_buf, page_table, w_cmp_k, w_cmp_v, slopes):
    q = p['q_nsa']
    Bd, Tn = q.shape[0], q.shape[1]
    n_pages = page_table.shape[1]
    past = n_pages * PAGE_SIZE

    def comp_page(pids):
        return _compress(cache_cmp_k[l, pids], w_cmp_k), _compress(cache_cmp_v[l, pids], w_cmp_v)

    kc_p, vc_p = lax.map(comp_page, page_table.T)
    to_rows = lambda a: jnp.moveaxis(a, 0, 1).reshape(Bd, past // CMP_BLOCK, NSA_KV_HEADS, HEAD_DIM)
    n_cn = Tn // CMP_BLOCK
    kc = jnp.concatenate([to_rows(kc_p), _compress(p['k_cmp'][:, :n_cn * CMP_BLOCK], w_cmp_k)], axis=1)
    vc = jnp.concatenate([to_rows(vc_p), _compress(p['v_cmp'][:, :n_cn * CMP_BLOCK], w_cmp_v)], axis=1)

    n_past_blk = past // SLC_BLOCK
    n_new_blk = -(-Tn // SLC_BLOCK)
    n_s = n_past_blk + n_new_blk
    bpp = PAGE_SIZE // SLC_BLOCK
    pad_n = ((0, 0), (0, n_new_blk * SLC_BLOCK - Tn), (0, 0), (0, 0))
    ks_nb = jnp.pad(p['k_slc'], pad_n).reshape(Bd, n_new_blk, SLC_BLOCK, NSA_KV_HEADS, HEAD_DIM)
    vs_nb = jnp.pad(p['v_slc'], pad_n).reshape(Bd, n_new_blk, SLC_BLOCK, NSA_KV_HEADS, HEAD_DIM)
    b_idx = jnp.arange(Bd)[:, None, None, None]
    g_idx = jnp.arange(NSA_KV_HEADS)[None, :, None, None]
    s_off = jnp.arange(SLC_BLOCK, dtype=jnp.int32)

    def gather_slc(sel):
        is_new = (sel >= n_past_blk)[..., None, None]
        jp = jnp.minimum(sel, n_past_blk - 1)
        page = page_table[b_idx, jp // bpp][..., None]
        row = (jp % bpp)[..., None] * SLC_BLOCK + s_off
        gp = g_idx[..., None]
        kp = cache_slc_k[l, page, row, gp]
        vp = cache_slc_v[l, page, row, gp]
        jn = jnp.clip(sel - n_past_blk, 0, n_new_blk - 1)
        kn = ks_nb[b_idx, jn, :, g_idx]
        vn = vs_nb[b_idx, jn, :, g_idx]
        return jnp.where(is_new, kn, kp), jnp.where(is_new, vn, vp)

    wk = jnp.concatenate([win_k_buf, p['k_win']], axis=1)
    wv = jnp.concatenate([win_v_buf, p['v_win']], axis=1)
    w_buf = win_k_buf.shape[1]
    w_pos = past - w_buf + jnp.arange(w_buf + Tn, dtype=jnp.int32)
    pos = past + jnp.arange(Tn, dtype=jnp.int32)
    o = _nsa_attend(q, p['gates'], pos, kc, vc, gather_slc, n_s, wk, wv, w_pos, slopes)
    keep = min(WINDOW, past + Tn)
    return o.reshape(Bd, Tn, NSA_HEADS * HEAD_DIM), wk[:, w_buf + Tn - keep:], wv[:, w_buf + Tn - keep:]


def _fox_prompt(q, k, v, logf):
    B, T = q.shape[0], q.shape[1]
    scale = HEAD_DIM ** -0.5
    c = jnp.cumsum(logf, axis=1).transpose(0, 2, 1)
    pos_k = jnp.arange(T, dtype=jnp.int32)

    def block(i):
        q0 = i * Q_BLOCK
        qb = lax.dynamic_slice_in_dim(q, q0, Q_BLOCK, axis=1)
        cq = lax.dynamic_slice_in_dim(c, q0, Q_BLOCK, axis=2)
        pos_q = q0 + jnp.arange(Q_BLOCK, dtype=jnp.int32)
        lg = jnp.einsum('bqhd,bkhd->bhqk', qb, k).astype(jnp.float32) * scale + cq[..., None] - c[:, :, None, :]
        pr = _masked_softmax(lg, pos_k[None, :] <= pos_q[:, None])
        return jnp.einsum('bhqk,bkhd->bqhd', pr.astype(v.dtype), v)

    o = lax.map(block, jnp.arange(T // Q_BLOCK, dtype=jnp.int32))
    return jnp.moveaxis(o, 0, 1).reshape(B, T, FOX_HEADS * HEAD_DIM)


def _fox_sample(q, k_new, v_new, logf_new, cache_k, cache_v, cache_logf, page_table, l):
    Bd, Tn = q.shape[0], q.shape[1]
    f32 = jnp.float32
    n_pages = page_table.shape[1]
    scale = HEAD_DIM ** -0.5
    lf_past = cache_logf[l, page_table].astype(f32).reshape(Bd, n_pages * PAGE_SIZE, FOX_HEADS)
    c_past = jnp.cumsum(lf_past, axis=1)
    c_new = c_past[:, -1:, :] + jnp.cumsum(logf_new, axis=1)
    cq = c_new.transpose(0, 2, 1)
    pos_n = jnp.arange(Tn, dtype=jnp.int32)
    mask = pos_n[None, :] <= pos_n[:, None]
    lg = jnp.einsum('bqhd,bkhd->bhqk', q, k_new).astype(f32) * scale + cq[..., None] - cq[:, :, None, :]
    lg = jnp.where(mask, lg, NEG_INF)
    m0 = jnp.max(lg, axis=-1, keepdims=True)
    p0 = jnp.where(mask, jnp.exp(lg - m0), 0.0)
    carry0 = (m0, p0.sum(-1, keepdims=True),
              jnp.einsum('bhqk,bkhd->bhqd', p0.astype(v_new.dtype), v_new).astype(f32))
    c_pages = jnp.moveaxis(c_past.reshape(Bd, n_pages, PAGE_SIZE, FOX_HEADS), 1, 0)

    def step(carry, xs):
        m, den, acc = carry
        pids, c_pg = xs
        kp = cache_k[l, pids]
        vp = cache_v[l, pids]
        lgp = jnp.einsum('bqhd,bkhd->bhqk', q, kp).astype(f32) * scale + cq[..., None] - c_pg.transpose(0, 2, 1)[:, :, None, :]
        m_new = jnp.maximum(m, jnp.max(lgp, axis=-1, keepdims=True))
        corr = jnp.exp(m - m_new)
        pr = jnp.exp(lgp - m_new)
        den = den * corr + pr.sum(-1, keepdims=True)
        acc = acc * corr + jnp.einsum('bhqk,bkhd->bhqd', pr.astype(vp.dtype), vp).astype(f32)
        return (m_new, den, acc), None

    (m, den, acc), _ = lax.scan(step, carry0, (page_table.T, c_pages))
    o = (acc / den).astype(q.dtype)
    return o.transpose(0, 2, 1, 3).reshape(Bd, Tn, FOX_HEADS * HEAD_DIM)


def _moe(x, w_router, b_router, w_eg, w_eu, w_ed, w_sg, w_su, w_sd):
    B, T, D = x.shape
    N = B * T
    f32 = jnp.float32
    x2 = x.reshape(N, D)
    s = jax.nn.sigmoid(jnp.einsum('nd,de->ne', x2, w_router).astype(f32))
    sb = s + b_router.astype(f32)
    grp = sb.reshape(N, N_EXPERT_GROUPS, N_EXPERTS // N_EXPERT_GROUPS)
    g_score = lax.top_k(grp, GROUP_SCORE_K)[0].sum(-1)
    _, g_sel = lax.top_k(g_score, TOPK_GROUPS)
    g_keep = jax.nn.one_hot(g_sel, N_EXPERT_GROUPS, dtype=f32).sum(axis=1) > 0
    e_keep = jnp.repeat(g_keep, N_EXPERTS // N_EXPERT_GROUPS, axis=1)
    _, idx = lax.top_k(jnp.where(e_keep, sb, NEG_INF), TOP_K)
    w = jnp.take_along_axis(s, idx, axis=1)
    w = ROUTED_SCALE * w / jnp.sum(w, axis=1, keepdims=True)
    blk = 128 if N * TOP_K >= 128 * N_EXPERTS else 8
    A = N * TOP_K
    e_flat = idx.reshape(A)
    tok = jnp.repeat(jnp.arange(N, dtype=jnp.int32), TOP_K)
    wt = w.reshape(A)
    order = jnp.argsort(e_flat)
    e_s, tok_s, wt_s = e_flat[order], tok[order], wt[order]
    counts = jnp.bincount(e_flat, length=N_EXPERTS)
    padded = (counts + blk - 1) // blk * blk
    pend = jnp.cumsum(padded)
    pstart = pend - padded
    ustart = jnp.cumsum(counts) - counts
    dest = pstart[e_s] + jnp.arange(A, dtype=jnp.int32) - ustart[e_s]
    n_blocks = -(-(A + N_EXPERTS * (blk - 1)) // blk)
    R = n_blocks * blk
    xb = jnp.zeros((R, D), x2.dtype).at[dest].set(x2[tok_s])
    blk_e = jnp.minimum(jnp.searchsorted(pend, jnp.arange(n_blocks, dtype=jnp.int32) * blk, side='right'), N_EXPERTS - 1)

    def expert_block(args):
        xs, e = args
        h = jax.nn.silu(xs @ w_eg[e]) * (xs @ w_eu[e])
        return h @ w_ed[e]

    yb = lax.map(expert_block, (xb.reshape(n_blocks, blk, D), blk_e)).reshape(R, D)
    routed = jax.ops.segment_sum(yb[dest].astype(f32) * wt_s[:, None], tok_s, num_segments=N)
    shared = (jax.nn.silu(x2 @ w_sg) * (x2 @ w_su)) @ w_sd
    return (routed.astype(x.dtype) + shared).reshape(B, T, D)


def setup_inputs(seed: int = 0) -> dict:
    key = jax.random.key(seed)
    ks = jax.random.split(key, 32)
    f32 = jnp.float32
    n_pages = PAST_LEN // PAGE_SIZE
    n_phys = (DEC_BATCH * n_pages * 5) // 4
    win_buf = min(WINDOW, PAST_LEN)
    nrm = lambda k, shape, s=1.0: s * jax.random.normal(k, shape, f32)
    col_scale = jnp.concatenate([jnp.full((wd,), DN_BETA if isv else 1.0, f32) for wd, isv in zip(PROJ_SPLITS, PROJ_IS_VALUE)])
    page_table = jax.random.permutation(ks[0], n_phys)[:DEC_BATCH * n_pages].reshape(DEC_BATCH, n_pages).astype(jnp.int32)
    fox_shape = (DEPTH, n_phys, PAGE_SIZE, FOX_HEADS, HEAD_DIM)
    nsa_shape = (DEPTH, n_phys, PAGE_SIZE, NSA_KV_HEADS, HEAD_DIM)
    win_shape = (DEPTH, DEC_BATCH, win_buf, NSA_KV_HEADS, HEAD_DIM)
    return {
        'x_prompt': nrm(ks[1], (BATCH, SEQ, D_MODEL)),
        'x_sample': nrm(ks[2], (DEC_BATCH, DEC_SEQ, D_MODEL)),
        'cache_fox_k': nrm(ks[3], fox_shape),
        'cache_fox_v': nrm(ks[4], fox_shape, DN_BETA),
        'cache_fox_logf': jax.nn.log_sigmoid(nrm(ks[5], (DEPTH, n_phys, PAGE_SIZE, FOX_HEADS)) + 2.5),
        'cache_cmp_k': nrm(ks[6], nsa_shape),
        'cache_cmp_v': nrm(ks[7], nsa_shape, DN_BETA),
        'cache_slc_k': nrm(ks[8], nsa_shape),
        'cache_slc_v': nrm(ks[9], nsa_shape, DN_BETA),
        'state_win_k': nrm(ks[10], win_shape),
        'state_win_v': nrm(ks[11], win_shape, DN_BETA),
        'page_table': page_table,
        'ln_in_g': 1.0 + nrm(ks[12], (D_MODEL,), 0.02),
        'ln_in_b': nrm(ks[13], (D_MODEL,), 0.02),
        'w_in': nrm(ks[14], (DEPTH, D_MODEL, D_PROJ), D_MODEL ** -0.5) * col_scale,
        'b_in': nrm(ks[15], (DEPTH, D_PROJ), 0.02).at[:, D_PROJ - FOX_HEADS:].add(2.5),
        'w_cmp_k': nrm(ks[16], (DEPTH, CMP_BLOCK, HEAD_DIM, HEAD_DIM), (CMP_BLOCK * HEAD_DIM) ** -0.5),
        'w_cmp_v': nrm(ks[17], (DEPTH, CMP_BLOCK, HEAD_DIM, HEAD_DIM), (CMP_BLOCK * HEAD_DIM) ** -0.5),
        'w_out': nrm(ks[18], (DEPTH, MIX_WIDTH, D_MODEL), DN_BETA * MIX_WIDTH ** -0.5),
        'ln1_g': 1.0 + nrm(ks[19], (DEPTH, D_MODEL), 0.02),
        'ln1_b': nrm(ks[20], (DEPTH, D_MODEL), 0.02),
        'w_router': nrm(ks[21], (DEPTH, D_MODEL, N_EXPERTS), D_MODEL ** -0.5),
        'b_router': nrm(ks[22], (DEPTH, N_EXPERTS), 0.01),
        'w_exp_gate': nrm(ks[23], (DEPTH, N_EXPERTS, D_MODEL, D_EXPERT), D_MODEL ** -0.5),
        'w_exp_up': nrm(ks[24], (DEPTH, N_EXPERTS, D_MODEL, D_EXPERT), D_MODEL ** -0.5),
        'w_exp_down': nrm(ks[25], (DEPTH, N_EXPERTS, D_EXPERT, D_MODEL), DN_BETA * D_EXPERT ** -0.5),
        'w_sh_gate': nrm(ks[26], (DEPTH, D_MODEL, D_SHARED), D_MODEL ** -0.5),
        'w_sh_up': nrm(ks[27], (DEPTH, D_MODEL, D_SHARED), D_MODEL ** -0.5),
        'w_sh_down': nrm(ks[28], (DEPTH, D_SHARED, D_MODEL), DN_BETA * D_SHARED ** -0.5),
        'ln2_g': 1.0 + nrm(ks[29], (DEPTH, D_MODEL), 0.02),
        'ln2_b': nrm(ks[30], (DEPTH, D_MODEL), 0.02),
    }


def reference(x_prompt, x_sample, cache_fox_k, cache_fox_v, cache_fox_logf, cache_cmp_k, cache_cmp_v,
              cache_slc_k, cache_slc_v, state_win_k, state_win_v, page_table, ln_in_g, ln_in_b, w_in, b_in,
              w_cmp_k, w_cmp_v, w_out, ln1_g, ln1_b, w_router, b_router, w_exp_gate, w_exp_up, w_exp_down,
              w_sh_gate, w_sh_up, w_sh_down, ln2_g, ln2_b):
    slopes = _alibi_slopes()
    xp = _layer_norm(x_prompt, ln_in_g, ln_in_b)
    xs = _layer_norm(x_sample, ln_in_g, ln_in_b)
    st = {name: [] for name in STATE_NAMES}
    for l in range(DEPTH):
        pp = _project(xp, w_in[l], b_in[l])
        ps = _project(xs, w_in[l], b_in[l])
        o_p = jnp.concatenate([_nsa_prompt(pp, w_cmp_k[l], w_cmp_v[l], slopes),
                               _fox_prompt(pp['q_fox'], pp['k_fox'], pp['v_fox'], pp['logf'])], axis=-1)
        o_ns, win_k_s, win_v_s = _nsa_sample(ps, l, cache_cmp_k, cache_cmp_v, cache_slc_k, cache_slc_v,
                                             state_win_k[l], state_win_v[l], page_table, w_cmp_k[l], w_cmp_v[l], slopes)
        o_fs = _fox_sample(ps['q_fox'], ps['k_fox'], ps['v_fox'], ps['logf'], cache_fox_k, cache_fox_v,
                           cache_fox_logf, page_table, l)
        o_s = jnp.concatenate([o_ns, o_fs], axis=-1)
        xp = _layer_norm(DN_ALPHA * xp + jnp.einsum('btm,md->btd', o_p, w_out[l]), ln1_g[l], ln1_b[l])
        xs = _layer_norm(DN_ALPHA * xs + jnp.einsum('btm,md->btd', o_s, w_out[l]), ln1_g[l], ln1_b[l])
        moe_args = (w_router[l], b_router[l], w_exp_gate[l], w_exp_up[l], w_exp_down[l], w_sh_gate[l], w_sh_up[l], w_sh_down[l])
        xp = _layer_norm(DN_ALPHA * xp + _moe(xp, *moe_args), ln2_g[l], ln2_b[l])
        xs = _layer_norm(DN_ALPHA * xs + _moe(xs, *moe_args), ln2_g[l], ln2_b[l])
        keep_p = min(WINDOW, pp['k_win'].shape[1])
        entries = (('p_fox_k', pp['k_fox']), ('p_fox_v', pp['v_fox']), ('p_fox_logf', pp['logf']),
                   ('p_cmp_k', pp['k_cmp']), ('p_cmp_v', pp['v_cmp']), ('p_slc_k', pp['k_slc']), ('p_slc_v', pp['v_slc']),
                   ('p_win_k', pp['k_win'][:, pp['k_win'].shape[1] - keep_p:]), ('p_win_v', pp['v_win'][:, pp['v_win'].shape[1] - keep_p:]),
                   ('s_fox_k', ps['k_fox']), ('s_fox_v', ps['v_fox']), ('s_fox_logf', ps['logf']),
                   ('s_cmp_k', ps['k_cmp']), ('s_cmp_v', ps['v_cmp']), ('s_slc_k', ps['k_slc']), ('s_slc_v', ps['v_slc']),
                   ('s_win_k', win_k_s), ('s_win_v', win_v_s))
        for name, arr in entries:
            st[name].append(arr)
    return (xp, xs,
            jnp.stack(st['p_fox_k']), jnp.stack(st['p_fox_v']), jnp.stack(st['p_fox_logf']),
            jnp.stack(st['p_cmp_k']), jnp.stack(st['p_cmp_v']), jnp.stack(st['p_slc_k']), jnp.stack(st['p_slc_v']),
            jnp.stack(st['p_win_k']), jnp.stack(st['p_win_v']),
            jnp.stack(st['s_fox_k']), jnp.stack(st['s_fox_v']), jnp.stack(st['s_fox_logf']),
            jnp.stack(st['s_cmp_k']), jnp.stack(st['s_cmp_v']), jnp.stack(st['s_slc_k']), jnp.stack(st['s_slc_v']),
            jnp.stack(st['s_win_k']), jnp.stack(st['s_win_v']))
```

```python
import functools

import numpy as np
import jax
import jax.numpy as jnp
from jax import lax
from jax.experimental import pallas as pl
from jax.experimental.pallas import tpu as pltpu

F32 = jnp.float32
BF16 = jnp.bfloat16
I32 = jnp.int32

D_MODEL = 1024
HEAD_DIM = 64
NSA_HEADS = 8
NSA_KV_HEADS = 2
NSA_GROUP = NSA_HEADS // NSA_KV_HEADS
FOX_HEADS = 8
CMP_BLOCK = 32
SLC_BLOCK = 64
SLC_TOPK = 16
WINDOW = 512
N_BRANCH = 3
FORCE_BONUS = 1000.0
N_EXPERTS = 64
N_EXPERT_GROUPS = 8
EXPERTS_PER_GROUP = N_EXPERTS // N_EXPERT_GROUPS
TOPK_GROUPS = 4
TOP_K = 4
D_EXPERT = 256
D_SHARED = 256
ROUTED_SCALE = 2.5
LN_EPS = 1e-5
DEPTH = 1
DN_ALPHA = (2.0 * DEPTH) ** 0.25
NEG_INF = -1e30
PAGE_SIZE = 128

LANES = 128
SUBLANES = 8
VMEM_LIMIT = 56 * 1024 * 1024

C_QN, C_QF, C_KF, C_VF = 0, 512, 1024, 1536
C_KC, C_VC, C_KS, C_VS, C_KW, C_VW, C_GL = 2048, 2176, 2304, 2432, 2560, 2688, 2816
N_PROJ = 2944
GL_LOGF = NSA_HEADS * N_BRANCH

R_QN, R_KC, R_VC, R_KS, R_VS, R_KW, R_VW, R_GN, R_QF, R_KF, R_VF, R_FF = (
    0, 512, 640, 768, 896, 1024, 1152, 1280, 1304, 1816, 2328, 2840)

NT_DIMS = (((1,), (1,)), ((), ()))


def _slope(g, r):
    return float(2.0 ** (-(g * NSA_GROUP + r + 1)))


def _nsa_col_perm():
    perm = np.zeros(NSA_HEADS * HEAD_DIM, np.int32)
    for j in range(NSA_GROUP):
        for g in range(NSA_KV_HEADS):
            for d in range(HEAD_DIM):
                perm[j * 128 + g * 64 + d] = (g * NSA_GROUP + j) * HEAD_DIM + d
    return perm


def _proj_perm():
    perm = np.full(N_PROJ, -1, np.int64)
    scale = np.ones(N_PROJ, np.float32)
    perm[C_QN:C_QN + 512] = R_QN + _nsa_col_perm()
    scale[C_QN:C_QN + 512] = HEAD_DIM ** -0.5
    perm[C_QF:C_QF + 512] = R_QF + np.arange(512)
    scale[C_QF:C_QF + 512] = HEAD_DIM ** -0.5
    perm[C_KF:C_KF + 512] = R_KF + np.arange(512)
    perm[C_VF:C_VF + 512] = R_VF + np.arange(512)
    for c, r in ((C_KC, R_KC), (C_VC, R_VC), (C_KS, R_KS), (C_VS, R_VS), (C_KW, R_KW), (C_VW, R_VW)):
        perm[c:c + 128] = r + np.arange(128)
    perm[C_GL:C_GL + 24] = R_GN + np.arange(24)
    perm[C_GL + 24:C_GL + 32] = R_FF + np.arange(8)
    return perm, scale


def _prep_proj_weights(w_in, b_in):
    perm, scale = _proj_perm()
    valid = jnp.asarray(perm >= 0)
    idx = jnp.asarray(np.maximum(perm, 0), I32)
    sc = jnp.asarray(scale)
    wp = jnp.where(valid[None, :], jnp.take(w_in, idx, axis=1) * sc[None, :], 0.0).astype(BF16)
    bp = jnp.where(valid, jnp.take(b_in, idx) * sc, 0.0).reshape(1, N_PROJ)
    return wp, bp


def _layer_norm(x, g, b):
    mu = jnp.mean(x, axis=-1, keepdims=True)
    xc = x - mu
    var = jnp.mean(xc * xc, axis=-1, keepdims=True)
    return xc * lax.rsqrt(var + LN_EPS) * g + b


def _split3(x):
    h = x.astype(BF16)
    r = x - h.astype(F32)
    m = r.astype(BF16)
    l = (r - m.astype(F32)).astype(BF16)
    return h, m, l


def _params(sem):
    return pltpu.CompilerParams(dimension_semantics=sem, vmem_limit_bytes=VMEM_LIMIT)


def _proj_body(x_ref, g_ref, b_ref, w_ref, bias_ref,
               qn_ref, qf_ref, kfb_ref, vfb_ref, ksb_ref, vsb_ref, kwb_ref, vwb_ref,
               fk_ref, fv_ref, ck_ref, cv_ref, sk_ref, sv_ref, wk_ref, wv_ref, gl_ref):
    xn = _layer_norm(x_ref[...], g_ref[...], b_ref[...])
    xb = xn.astype(BF16)

    def sec(c, n):
        return jnp.dot(xb, w_ref[:, c:c + n], preferred_element_type=F32) + bias_ref[:, c:c + n]

    qn_ref[...] = sec(C_QN, 512).astype(BF16)
    qf_ref[...] = sec(C_QF, 512).astype(BF16)
    h = sec(C_KF, 512)
    fk_ref[...] = h
    kfb_ref[...] = h.astype(BF16)
    h = sec(C_VF, 512)
    fv_ref[...] = h
    vfb_ref[...] = h.astype(BF16)
    ck_ref[...] = sec(C_KC, 128)
    cv_ref[...] = sec(C_VC, 128)
    for c, f_ref, b_ref2 in ((C_KS, sk_ref, ksb_ref), (C_VS, sv_ref, vsb_ref),
                             (C_KW, wk_ref, kwb_ref), (C_VW, wv_ref, vwb_ref)):
        h = sec(c, 128)
        f_ref[...] = h
        b_ref2[...] = h.astype(BF16)
    h = sec(C_GL, 128)
    lane = lax.broadcasted_iota(I32, h.shape, 1)
    gl_ref[...] = jnp.where(lane < GL_LOGF, jax.nn.sigmoid(h), jax.nn.log_sigmoid(h))


def _project(x2, ln_g, ln_b, wp, bp, tm):
    n = x2.shape[0]
    bf = lambda w: jax.ShapeDtypeStruct((n, w), BF16)
    ff = lambda w: jax.ShapeDtypeStruct((n, w), F32)
    out_shape = [bf(512), bf(512), bf(512), bf(512), bf(128), bf(128), bf(128), bf(128),
                 ff(512), ff(512), ff(128), ff(128), ff(128), ff(128), ff(128), ff(128), ff(128)]
    row = lambda w: pl.BlockSpec((tm, w), lambda i: (i, 0))
    full = lambda a: pl.BlockSpec(a.shape, lambda i: (0,) * a.ndim)
    return pl.pallas_call(
        _proj_body,
        grid=(n // tm,),
        in_specs=[row(D_MODEL), full(ln_g), full(ln_b), full(wp), full(bp)],
        out_specs=[row(s.shape[1]) for s in out_shape],
        out_shape=out_shape,
        compiler_params=_params(("parallel",)),
        name="ln_in_proj",
    )(x2, ln_g, ln_b, wp, bp)


def _compress_body(xk_ref, xv_ref, wk_ref, wv_ref, kc_ref, vc_ref):
    half = kc_ref.shape[1] // 2
    for x_ref, w_ref, o_ref in ((xk_ref, wk_ref, kc_ref), (xv_ref, wv_ref, vc_ref)):
        for par in range(2):
            acc = jnp.zeros((half, 128), F32)
            for i in range(CMP_BLOCK):
                rows = x_ref[pl.ds(par * CMP_BLOCK + i, half, stride=2 * CMP_BLOCK), :]
                acc = acc + jnp.dot(rows.astype(BF16), w_ref[i], preferred_element_type=F32)
            o_ref[0, par * half:(par + 1) * half, :] = acc.astype(BF16)


def _compress_prompt(ck, cv, wck, wcv, batch, seq):
    n_c = seq // CMP_BLOCK
    rows = pl.BlockSpec((seq, 128), lambda b: (b, 0))
    wsp = pl.BlockSpec((CMP_BLOCK, 128, 128), lambda b: (0, 0, 0))
    osp = pl.BlockSpec((1, n_c, 128), lambda b: (b, 0, 0))
    return pl.pallas_call(
        _compress_body,
        grid=(batch,),
        in_specs=[rows, rows, wsp, wsp],
        out_specs=[osp, osp],
        out_shape=[jax.ShapeDtypeStruct((batch, n_c, 128), BF16)] * 2,
        compiler_params=_params(("parallel",)),
        name="compress_prompt",
    )(ck, cv, wck, wcv)


def _blockdiag_cmp_weight(w):
    z = jnp.zeros_like(w)
    top = jnp.concatenate([w, z], axis=2)
    bot = jnp.concatenate([z, w], axis=2)
    return jnp.concatenate([top, bot], axis=1).astype(BF16)


NSA_TQ = 128
NSA_TK = 512


def _top16_mask(score, lane_f):
    sel = jnp.zeros(score.shape, F32)
    s = score
    for _ in range(SLC_TOPK):
        m = jnp.max(s, axis=-1, keepdims=True)
        idx = jnp.min(jnp.where(s == m, lane_f, float(LANES)), axis=-1, keepdims=True)
        hit = lane_f == idx
        sel = jnp.where(hit, 1.0, sel)
        s = jnp.where(hit, -jnp.inf, s)
    return sel


def _nsa_prompt_body(q_ref, gl_ref, kc_ref, vc_ref, ks_ref, vs_ref, kw_ref, vw_ref, o_ref,
                     qs_ref, m_ref, l_ref, acc_ref):
    i = pl.program_id(1)
    q0 = i * NSA_TQ
    n_c = kc_ref.shape[1]
    lane = lax.broadcasted_iota(I32, (NSA_TQ, LANES), 1)
    row = lax.broadcasted_iota(I32, (NSA_TQ, LANES), 0)
    lo = lane < HEAD_DIM
    zero_b = jnp.zeros((NSA_TQ, LANES), BF16)
    for j in range(NSA_GROUP):
        qj = q_ref[:, j * 128:(j + 1) * 128]
        qs_ref[2 * j] = jnp.where(lo, qj, zero_b)
        qs_ref[2 * j + 1] = jnp.where(lo, zero_b, qj)
    qs = qs_ref[...].reshape(8 * NSA_TQ, LANES)
    slopes = [_slope(rb % 2, rb // 2) for rb in range(8)]

    lc_all = lax.dot_general(qs, kc_ref[0], NT_DIMS, preferred_element_type=F32)
    col = lax.broadcasted_iota(I32, (NSA_TQ, n_c), 1)
    half = n_c // 2
    cblk = 2 * (col % half) + col // half
    pos_c = q0 + lax.broadcasted_iota(I32, (NSA_TQ, n_c), 0)
    dist_c = (pos_c - (cblk * CMP_BLOCK + (CMP_BLOCK - 1))).astype(F32)
    mask_c = dist_c >= 0
    psum = [jnp.zeros((NSA_TQ, n_c), F32), jnp.zeros((NSA_TQ, n_c), F32)]
    pcs = []
    for rb in range(8):
        lc = lc_all[rb * NSA_TQ:(rb + 1) * NSA_TQ] - slopes[rb] * dist_c
        lc = jnp.where(mask_c, lc, NEG_INF)
        m = jnp.max(lc, axis=-1, keepdims=True)
        p = jnp.where(mask_c, jnp.exp(lc - m), 0.0)
        pc = p / jnp.maximum(jnp.sum(p, axis=-1, keepdims=True), 1e-30)
        psum[rb % 2] = psum[rb % 2] + pc
        pcs.append(pc.astype(BF16))
    o_cmp = jnp.dot(jnp.concatenate(pcs, axis=0), vc_ref[0], preferred_element_type=F32)

    lane_f = lane.astype(F32)
    pos = q0 + row
    forced = (lane == 0) | (lane == pos // SLC_BLOCK)
    visible = lane * SLC_BLOCK <= pos
    sels = []
    for g in range(NSA_KV_HEADS):
        imp = psum[g][:, :half] + psum[g][:, half:]
        score = jnp.where(visible, imp + jnp.where(forced, FORCE_BONUS, 0.0), -1.0)
        sels.append(_top16_mask(score, lane_f))
    selstack = jnp.concatenate(sels, axis=0).astype(BF16)

    m_ref[...] = jnp.full(m_ref.shape, NEG_INF, F32)
    l_ref[...] = jnp.zeros(l_ref.shape, F32)
    acc_ref[...] = jnp.zeros(acc_ref.shape, F32)
    n_tiles = (q0 + NSA_TQ + NSA_TK - 1) // NSA_TK
    blk_per_tile = NSA_TK // SLC_BLOCK
    e_row = lax.broadcasted_iota(I32, (LANES, NSA_TK), 0)
    e_col = lax.broadcasted_iota(I32, (LANES, NSA_TK), 1) // SLC_BLOCK
    k_lane = lax.broadcasted_iota(I32, (NSA_TQ, NSA_TK), 1)
    q_row = lax.broadcasted_iota(I32, (NSA_TQ, NSA_TK), 0)

    def slc_tile(t, carry):
        k0 = pl.multiple_of(t * NSA_TK, NSA_TK)
        kt = ks_ref[0, pl.ds(k0, NSA_TK), :]
        vt = vs_ref[0, pl.ds(k0, NSA_TK), :]
        s_all = lax.dot_general(qs, kt, NT_DIMS, preferred_element_type=F32)
        expand = jnp.where(e_row == e_col + t * blk_per_tile, 1.0, 0.0).astype(BF16)
        mexp = jnp.dot(selstack, expand, preferred_element_type=F32)
        dist_i = (q0 + q_row) - (k0 + k_lane)
        dist = dist_i.astype(F32)
        oks = [jnp.where(dist_i >= 0, mexp[g * NSA_TQ:(g + 1) * NSA_TQ], 0.0) > 0.5
               for g in range(NSA_KV_HEADS)]
        for rb in range(8):
            ok = oks[rb % 2]
            rs = slice(rb * NSA_TQ, (rb + 1) * NSA_TQ)
            s = jnp.where(ok, s_all[rs] - slopes[rb] * dist, NEG_INF)
            m_old = m_ref[rs]
            m_new = jnp.maximum(m_old, jnp.max(s, axis=-1, keepdims=True))
            alpha = jnp.exp(m_old - m_new)
            p = jnp.where(ok, jnp.exp(s - m_new), 0.0)
            l_ref[rs] = alpha * l_ref[rs] + jnp.sum(p, axis=-1, keepdims=True)
            acc_ref[rs] = alpha * acc_ref[rs] + jnp.dot(p.astype(BF16), vt, preferred_element_type=F32)
            m_ref[rs] = m_new
        return carry

    lax.fori_loop(0, n_tiles, slc_tile, 0)

    tw = WINDOW + NSA_TQ
    ws = pl.multiple_of(jnp.maximum(q0 - WINDOW, 0), NSA_TQ)
    kwt = kw_ref[0, pl.ds(ws, tw), :]
    vwt = vw_ref[0, pl.ds(ws, tw), :]
    sw_all = lax.dot_general(qs, kwt, NT_DIMS, preferred_element_type=F32)
    dist_wi = (q0 + lax.broadcasted_iota(I32, (NSA_TQ, tw), 0)) - (ws + lax.broadcasted_iota(I32, (NSA_TQ, tw), 1))
    dist_w = dist_wi.astype(F32)
    ok_w = jnp.where(dist_wi >= 0, jnp.where(dist_wi < WINDOW, 1.0, 0.0), 0.0) > 0.5
    pws = []
    for rb in range(8):
        s = jnp.where(ok_w, sw_all[rb * NSA_TQ:(rb + 1) * NSA_TQ] - slopes[rb] * dist_w, NEG_INF)
        m = jnp.max(s, axis=-1, keepdims=True)
        p = jnp.where(ok_w, jnp.exp(s - m), 0.0)
        pw = p / jnp.maximum(jnp.sum(p, axis=-1, keepdims=True), 1e-30)
        pws.append(pw.astype(BF16))
    o_win = jnp.dot(jnp.concatenate(pws, axis=0), vwt, preferred_element_type=F32)

    gl = gl_ref[...]
    outs = []
    for rb in range(8):
        g, r = rb % 2, rb // 2
        c0 = g * NSA_GROUP * N_BRANCH + r * N_BRANCH
        rs = slice(rb * NSA_TQ, (rb + 1) * NSA_TQ)
        o_slc = acc_ref[rs] / jnp.maximum(l_ref[rs], 1e-30)
        outs.append(gl[:, c0:c0 + 1] * o_cmp[rs] + gl[:, c0 + 1:c0 + 2] * o_slc
                    + gl[:, c0 + 2:c0 + 3] * o_win[rs])
    for j in range(NSA_GROUP):
        o_ref[:, j * 128:(j + 1) * 128] = jnp.where(lo, outs[2 * j], outs[2 * j + 1]).astype(BF16)


def _nsa_prompt(qn, gl, kc, vc, ksb, vsb, kwb, vwb, batch, seq):
    nq = seq // NSA_TQ
    n_c = seq // CMP_BLOCK
    qsp = pl.BlockSpec((NSA_TQ, 512), lambda b, i: (b * nq + i, 0))
    glsp = pl.BlockSpec((NSA_TQ, 128), lambda b, i: (b * nq + i, 0))
    csp = pl.BlockSpec((1, n_c, 128), lambda b, i: (b, 0, 0))
    seqsp = pl.BlockSpec((1, seq, 128), lambda b, i: (b, 0, 0))
    r3 = lambda a: a.reshape(batch, seq, 128)
    return pl.pallas_call(
        _nsa_prompt_body,
        grid=(batch, nq),
        in_specs=[qsp, glsp, csp, csp, seqsp, seqsp, seqsp, seqsp],
        out_specs=qsp,
        out_shape=jax.ShapeDtypeStruct((batch * seq, 512), BF16),
        scratch_shapes=[pltpu.VMEM((8, NSA_TQ, LANES), BF16),
                        pltpu.VMEM((8 * NSA_TQ, 1), F32),
                        pltpu.VMEM((8 * NSA_TQ, 1), F32),
                        pltpu.VMEM((8 * NSA_TQ, LANES), F32)],
        compiler_params=_params(("parallel", "arbitrary")),
        name="nsa_prompt",
    )(qn, gl, kc, vc, r3(ksb), r3(vsb), r3(kwb), r3(vwb))


CS_CHUNK = 256


def _cumsum_body(gl_ref, ccol_ref, crow_ref, carry_ref):
    @pl.when(pl.program_id(1) == 0)
    def _():
        carry_ref[...] = jnp.zeros(carry_ref.shape, F32)

    r = lax.broadcasted_iota(I32, (CS_CHUNK, CS_CHUNK), 0)
    c = lax.broadcasted_iota(I32, (CS_CHUNK, CS_CHUNK), 1)
    tri = jnp.where(c <= r, 1.0, 0.0).astype(BF16)
    h, m, l = _split3(gl_ref[...])
    cs = (jnp.dot(tri, h, preferred_element_type=F32) + jnp.dot(tri, m, preferred_element_type=F32)
          + jnp.dot(tri, l, preferred_element_type=F32)) + carry_ref[0:1, :]
    ccol_ref[...] = cs
    crow_ref[0] = cs.T[GL_LOGF:GL_LOGF + FOX_HEADS, :]
    carry_ref[...] = jnp.broadcast_to(cs[CS_CHUNK - 1:CS_CHUNK, :], carry_ref.shape)


def _logf_cumsum(gl, batch, seq):
    nch = seq // CS_CHUNK
    return pl.pallas_call(
        _cumsum_body,
        grid=(batch, nch),
        in_specs=[pl.BlockSpec((CS_CHUNK, 128), lambda b, i: (b * nch + i, 0))],
        out_specs=[pl.BlockSpec((CS_CHUNK, 128), lambda b, i: (b * nch + i, 0)),
                   pl.BlockSpec((1, FOX_HEADS, CS_CHUNK), lambda b, i: (b, 0, i))],
        out_shape=[jax.ShapeDtypeStruct((batch * seq, 128), F32),
                   jax.ShapeDtypeStruct((batch, FOX_HEADS, seq), F32)],
        scratch_shapes=[pltpu.VMEM((SUBLANES, 128), F32)],
        compiler_params=_params(("parallel", "arbitrary")),
        name="logf_cumsum",
    )(gl)


FOX_T = 256


def _fox_prompt_body(q_ref, k_ref, v_ref, ccol_ref, crow_ref, o_ref, m_ref, l_ref, acc_ref):
    qi = pl.program_id(1)
    lane = lax.broadcasted_iota(I32, (FOX_T, LANES), 1)
    lo = lane < HEAD_DIM
    zero_b = jnp.zeros((FOX_T, LANES), BF16)
    causal = lax.broadcasted_iota(I32, (FOX_T, FOX_T), 1) <= lax.broadcasted_iota(I32, (FOX_T, FOX_T), 0)
    ccol = ccol_ref[...]
    for hp in range(FOX_HEADS // 2):
        q = q_ref[:, hp * 128:(hp + 1) * 128]
        qs = jnp.concatenate([jnp.where(lo, q, zero_b), jnp.where(lo, zero_b, q)], axis=0)
        cqs = [ccol[:, GL_LOGF + 2 * hp + e:GL_LOGF + 2 * hp + e + 1] for e in range(2)]
        m_ref[...] = jnp.full(m_ref.shape, NEG_INF, F32)
        l_ref[...] = jnp.zeros(l_ref.shape, F32)
        acc_ref[...] = jnp.zeros(acc_ref.shape, F32)

        def tile(t, masked):
            k0 = pl.multiple_of(t * FOX_T, FOX_T)
            kt = k_ref[pl.ds(k0, FOX_T), hp * 128:(hp + 1) * 128]
            vt = v_ref[pl.ds(k0, FOX_T), hp * 128:(hp + 1) * 128]
            s_all = lax.dot_general(qs, kt, NT_DIMS, preferred_element_type=F32)
            for e in range(2):
                rs = slice(e * FOX_T, (e + 1) * FOX_T)
                ck = crow_ref[0, 2 * hp + e:2 * hp + e + 1, pl.ds(k0, FOX_T)]
                s = (s_all[rs] + cqs[e]) - ck
                if masked:
                    s = jnp.where(causal, s, NEG_INF)
                m_old = m_ref[rs]
                m_new = jnp.maximum(m_old, jnp.max(s, axis=-1, keepdims=True))
                alpha = jnp.exp(m_old - m_new)
                p = jnp.exp(s - m_new)
                if masked:
                    p = jnp.where(causal, p, 0.0)
                l_ref[rs] = alpha * l_ref[rs] + jnp.sum(p, axis=-1, keepdims=True)
                acc_ref[rs] = alpha * acc_ref[rs] + jnp.dot(p.astype(BF16), vt, preferred_element_type=F32)
                m_ref[rs] = m_new

        def full_tile(t, carry):
            tile(t, False)
            return carry

        lax.fori_loop(0, qi, full_tile, 0)
        tile(qi, True)
        o = acc_ref[...] / l_ref[...]
        o_ref[:, hp * 128:(hp + 1) * 128] = jnp.where(lo, o[:FOX_T], o[FOX_T:]).astype(BF16)


def _fox_prompt(qf, kfb, vfb, ccol, crow, batch, seq):
    nq = seq // FOX_T
    qsp = pl.BlockSpec((FOX_T, 512), lambda b, i: (b * nq + i, 0))
    kvsp = pl.BlockSpec((seq, 512), lambda b, i: (b, 0))
    return pl.pallas_call(
        _fox_prompt_body,
        grid=(batch, nq),
        in_specs=[qsp, kvsp, kvsp,
                  pl.BlockSpec((FOX_T, 128), lambda b, i: (b * nq + i, 0)),
                  pl.BlockSpec((1, FOX_HEADS, seq), lambda b, i: (b, 0, 0))],
        out_specs=qsp,
        out_shape=jax.ShapeDtypeStruct((batch * seq, 512), BF16),
        scratch_shapes=[pltpu.VMEM((2 * FOX_T, 1), F32),
                        pltpu.VMEM((2 * FOX_T, 1), F32),
                        pltpu.VMEM((2 * FOX_T, LANES), F32)],
        compiler_params=_params(("parallel", "arbitrary")),
        name="fox_prompt",
    )(qf, kfb, vfb, ccol, crow)


def _outproj_body(on_ref, of_ref, x_ref, g0_ref, b0_ref, wn_ref, wf_ref, g1_ref, b1_ref, y_ref):
    xn = _layer_norm(x_ref[...], g0_ref[...], b0_ref[...])
    a = (jnp.dot(on_ref[...], wn_ref[...], preferred_element_type=F32)
         + jnp.dot(of_ref[...], wf_ref[...], preferred_element_type=F32))
    y_ref[...] = _layer_norm(DN_ALPHA * xn + a, g1_ref[...], b1_ref[...])


def _outproj_ln1(o_n, o_f, x2, g0, b0, w_n, w_f, g1, b1, tm):
    n = x2.shape[0]
    full = lambda a: pl.BlockSpec(a.shape, lambda i: (0,) * a.ndim)
    row = lambda w: pl.BlockSpec((tm, w), lambda i: (i, 0))
    return pl.pallas_call(
        _outproj_body,
        grid=(n // tm,),
        in_specs=[row(512), row(512), row(D_MODEL), full(g0), full(b0), full(w_n), full(w_f), full(g1), full(b1)],
        out_specs=row(D_MODEL),
        out_shape=jax.ShapeDtypeStruct((n, D_MODEL), F32),
        compiler_params=_params(("parallel",)),
        name="outproj_ln1",
    )(o_n, o_f, x2, g0, b0, w_n, w_f, g1, b1)


def _first_max(v, idx_f, big, axes):
    m = v
    for ax in axes:
        m = jnp.max(m, axis=ax, keepdims=True)
    i = jnp.where(v == m, idx_f, big)
    for ax in axes:
        i = jnp.min(i, axis=ax, keepdims=True)
    return m, i


def _router_body(x_ref, wr_ref, br_ref, e4_ref, p4_ref, w4_ref, cnt_ref, carry_ref):
    tm = x_ref.shape[0]

    @pl.when(pl.program_id(0) == 0)
    def _():
        carry_ref[...] = jnp.zeros(carry_ref.shape, F32)

    xb = x_ref[...].astype(BF16)
    logits = lax.dot_general(wr_ref[...], xb, NT_DIMS, preferred_element_type=F32)
    s = jax.nn.sigmoid(logits)
    sb = s + br_ref[...]
    ng, ne = N_EXPERT_GROUPS, EXPERTS_PER_GROUP
    s3 = s.reshape(ng, ne, tm)
    sb3 = sb.reshape(ng, ne, tm)
    e_idx = lax.broadcasted_iota(I32, (ng, ne, tm), 1).astype(F32)
    g_idx3 = lax.broadcasted_iota(I32, (ng, ne, tm), 0).astype(F32)
    flat = g_idx3 * ne + e_idx
    m1, i1 = _first_max(sb3, e_idx, float(ne), (1,))
    m2, _ = _first_max(jnp.where(e_idx == i1, -jnp.inf, sb3), e_idx, float(ne), (1,))
    gs = (m1 + m2)
    g_idx = lax.broadcasted_iota(I32, (ng, 1, tm), 0).astype(F32)
    keep = jnp.zeros((ng, 1, tm), F32)
    for _ in range(TOPK_GROUPS):
        _, gi = _first_max(gs, g_idx, float(ng), (0,))
        hit = g_idx == gi
        keep = jnp.where(hit, 1.0, keep)
        gs = jnp.where(hit, -jnp.inf, gs)
    cand = jnp.where(keep > 0.5, sb3, NEG_INF)
    hits, es, ws = [], [], []
    for _ in range(TOP_K):
        _, fi = _first_max(cand, flat, float(N_EXPERTS), (1, 0))
        hit = flat == fi
        cand = jnp.where(hit, -jnp.inf, cand)
        hits.append(jnp.where(hit, 1.0, 0.0))
        es.append(fi.reshape(1, tm))
        ws.append(jnp.sum(jnp.sum(jnp.where(hit, s3, 0.0), axis=1, keepdims=True), axis=0, keepdims=True).reshape(1, tm))
    wsum = ((ws[0] + ws[1]) + ws[2]) + ws[3]
    sel = (hits[0] + hits[1] + hits[2] + hits[3]).reshape(N_EXPERTS, tm)
    r = lax.broadcasted_iota(I32, (tm, tm), 0)
    c = lax.broadcasted_iota(I32, (tm, tm), 1)
    strict = jnp.where(r < c, 1.0, 0.0).astype(BF16)
    pos = jnp.dot(sel.astype(BF16), strict, preferred_element_type=F32) + carry_ref[:, 0:1]
    pos3 = pos.reshape(ng, ne, tm)
    for k in range(TOP_K):
        e4_ref[k:k + 1, :] = es[k].astype(I32)
        pk = jnp.sum(jnp.sum(hits[k] * pos3, axis=1, keepdims=True), axis=0, keepdims=True).reshape(1, tm)
        p4_ref[k:k + 1, :] = pk.astype(I32)
        w4_ref[k:k + 1, :] = ROUTED_SCALE * ws[k] / wsum
    total = carry_ref[...] + jnp.sum(sel, axis=1, keepdims=True)
    carry_ref[...] = total
    cnt_ref[...] = total.astype(I32)


def _router(x1, wr_t, br_col, tm):
    n = x1.shape[0]
    osp = pl.BlockSpec((TOP_K, tm), lambda i: (0, i))
    return pl.pallas_call(
        _router_body,
        grid=(n // tm,),
        in_specs=[pl.BlockSpec((tm, D_MODEL), lambda i: (i, 0)),
                  pl.BlockSpec(wr_t.shape, lambda i: (0, 0)),
                  pl.BlockSpec(br_col.shape, lambda i: (0, 0))],
        out_specs=[osp, osp, osp, pl.BlockSpec((N_EXPERTS, LANES), lambda i: (0, 0))],
        out_shape=[jax.ShapeDtypeStruct((TOP_K, n), I32), jax.ShapeDtypeStruct((TOP_K, n), I32),
                   jax.ShapeDtypeStruct((TOP_K, n), F32), jax.ShapeDtypeStruct((N_EXPERTS, LANES), I32)],
        scratch_shapes=[pltpu.VMEM((N_EXPERTS, LANES), F32)],
        compiler_params=_params(("arbitrary",)),
        name="moe_router",
    )(x1, wr_t, br_col)


def _row_copy(src, s, dst, d, sem):
    return pltpu.make_async_copy(src.at[pl.ds(s, 1)], dst.at[pl.ds(d, 1)], sem)


def _dispatch_body(dest_ref, x_ref, xb_in_ref, xb_ref, sem):
    del xb_in_ref
    tm = dest_ref.shape[1]
    base = pl.program_id(0) * tm

    def start(t, carry):
        for k in range(TOP_K):
            _row_copy(x_ref, base + t, xb_ref, dest_ref[k, t], sem).start()
        return carry

    lax.fori_loop(0, tm, start, 0)

    def wait(t, carry):
        for k in range(TOP_K):
            _row_copy(x_ref, 0, xb_ref, 0, sem).wait()
        return carry

    lax.fori_loop(0, tm, wait, 0)


def _dispatch(dest4, x1, n_rows, tm):
    n = x1.shape[0]
    xb0 = jnp.zeros((n_rows, D_MODEL), F32)
    return pl.pallas_call(
        _dispatch_body,
        grid=(n // tm,),
        in_specs=[pl.BlockSpec((TOP_K, tm), lambda i: (0, i), memory_space=pltpu.SMEM),
                  pl.BlockSpec(memory_space=pl.ANY),
                  pl.BlockSpec(memory_space=pl.ANY)],
        out_specs=pl.BlockSpec(memory_space=pl.ANY),
        out_shape=jax.ShapeDtypeStruct((n_rows, D_MODEL), F32),
        scratch_shapes=[pltpu.SemaphoreType.DMA(())],
        input_output_aliases={2: 0},
        compiler_params=pltpu.CompilerParams(dimension_semantics=("arbitrary",), has_side_effects=True,
                                             vmem_limit_bytes=VMEM_LIMIT),
        name="moe_dispatch",
    )(dest4, x1, xb0)


def _expert_body(blk_e_ref, n_used_ref, x_ref, wg_ref, wu_ref, wd_ref, y_ref):
    del blk_e_ref
    i = pl.program_id(0)

    @pl.when(i < n_used_ref[0])
    def _():
        xs = x_ref[...].astype(BF16)
        hg = jnp.dot(xs, wg_ref[0].astype(BF16), preferred_element_type=F32)
        hu = jnp.dot(xs, wu_ref[0].astype(BF16), preferred_element_type=F32)
        h = (jax.nn.silu(hg) * hu).astype(BF16)
        y_ref[...] = jnp.dot(h, wd_ref[0].astype(BF16), preferred_element_type=F32)

    @pl.when(i >= n_used_ref[0])
    def _():
        y_ref[...] = jnp.zeros(y_ref.shape, F32)


def _experts(blk_e, n_used, xb, w_g, w_u, w_d, blk):
    n_rows = xb.shape[0]
    grid_spec = pltpu.PrefetchScalarGridSpec(
        num_scalar_prefetch=2,
        grid=(n_rows // blk,),
        in_specs=[pl.BlockSpec((blk, D_MODEL), lambda i, be, nu: (i, 0)),
                  pl.BlockSpec((1, D_MODEL, D_EXPERT), lambda i, be, nu: (be[i], 0, 0)),
                  pl.BlockSpec((1, D_MODEL, D_EXPERT), lambda i, be, nu: (be[i], 0, 0)),
                  pl.BlockSpec((1, D_EXPERT, D_MODEL), lambda i, be, nu: (be[i], 0, 0))],
        out_specs=pl.BlockSpec((blk, D_MODEL), lambda i, be, nu: (i, 0)),
    )
    return pl.pallas_call(
        _expert_body,
        grid_spec=grid_spec,
        out_shape=jax.ShapeDtypeStruct((n_rows, D_MODEL), F32),
        compiler_params=_params(("arbitrary",)),
        name="moe_experts",
    )(blk_e, n_used, xb, w_g, w_u, w_d)


def _combine_body(dest_ref, w_ref, x_ref, yb_ref, wsg_ref, wsu_ref, wsd_ref, g_ref, b_ref, y_ref, gbuf, sem):
    tm = x_ref.shape[0]

    def start(t, carry):
        for k in range(TOP_K):
            pltpu.make_async_copy(yb_ref.at[pl.ds(dest_ref[k, t], 1)], gbuf.at[k, pl.ds(t, 1)], sem).start()
        return carry

    lax.fori_loop(0, tm, start, 0)
    x = x_ref[...]
    xb = x.astype(BF16)
    hs = jax.nn.silu(jnp.dot(xb, wsg_ref[...], preferred_element_type=F32)) * jnp.dot(xb, wsu_ref[...], preferred_element_type=F32)
    shared = jnp.dot(hs.astype(BF16), wsd_ref[...], preferred_element_type=F32)

    def wait(t, carry):
        for k in range(TOP_K):
            pltpu.make_async_copy(yb_ref.at[pl.ds(0, 1)], gbuf.at[k, pl.ds(0, 1)], sem).wait()
        return carry

    lax.fori_loop(0, tm, wait, 0)
    w = w_ref[...]
    routed = gbuf[0] * w[:, 0:1]
    for k in range(1, TOP_K):
        routed = routed + gbuf[k] * w[:, k:k + 1]
    y_ref[...] = _layer_norm(DN_ALPHA * x + (routed + shared), g_ref[...], b_ref[...])


def _combine(dest4, w4t, x1, yb, wsg, wsu, wsd, g2, b2, tm):
    n = x1.shape[0]
    full = lambda a: pl.BlockSpec(a.shape, lambda i: (0,) * a.ndim)
    return pl.pallas_call(
        _combine_body,
        grid=(n // tm,),
        in_specs=[pl.BlockSpec((TOP_K, tm), lambda i: (0, i), memory_space=pltpu.SMEM),
                  pl.BlockSpec((tm, TOP_K), lambda i: (i, 0)),
                  pl.BlockSpec((tm, D_MODEL), lambda i: (i, 0)),
                  pl.BlockSpec(memory_space=pl.ANY),
                  full(wsg), full(wsu), full(wsd), full(g2), full(b2)],
        out_specs=pl.BlockSpec((tm, D_MODEL), lambda i: (i, 0)),
        out_shape=jax.ShapeDtypeStruct((n, D_MODEL), F32),
        scratch_shapes=[pltpu.VMEM((TOP_K, tm, D_MODEL), F32), pltpu.SemaphoreType.DMA(())],
        compiler_params=_params(("arbitrary",)),
        name="moe_combine",
    )(dest4, w4t, x1, yb, wsg, wsu, wsd, g2, b2)


def _moe_ln2(x1, wr_t, br_col, w_g, w_u, w_d, wsg, wsu, wsd, g2, b2, tm, blk):
    n = x1.shape[0]
    e4, p4, w4, cnt = _router(x1, wr_t, br_col, min(tm, 128))
    counts = cnt[:, 0]
    padded = (counts + blk - 1) // blk * blk
    pend = jnp.cumsum(padded)
    pstart = pend - padded
    dest4 = pstart[e4] + p4
    n_blocks = -(-(n * TOP_K + N_EXPERTS * (blk - 1)) // blk)
    blk_e = jnp.minimum(jnp.searchsorted(pend, jnp.arange(n_blocks, dtype=I32) * blk, side='right'),
                        N_EXPERTS - 1).astype(I32)
    n_used = (pend[-1:] // blk).astype(I32)
    xb = _dispatch(dest4, x1, n_blocks * blk, tm)
    yb = _experts(blk_e, n_used, xb, w_g, w_u, w_d, blk)
    return _combine(dest4, w4.T, x1, yb, wsg, wsu, wsd, g2, b2, tm)


PC_SLABS = 256


def _paged_compress_body(x_ref, w_ref, o_ref):
    acc = jnp.zeros((PC_SLABS, 4 * HEAD_DIM), F32)
    for dp in range(HEAD_DIM // 2):
        a = x_ref[pl.ds(2 * dp, PC_SLABS, stride=HEAD_DIM), :]
        b = x_ref[pl.ds(2 * dp + 1, PC_SLABS, stride=HEAD_DIM), :]
        lhs = jnp.concatenate([a, b], axis=1).astype(BF16)
        acc = acc + jnp.dot(lhs, w_ref[dp], preferred_element_type=F32)
    o_ref[...] = acc


def _paged_compress(cache_t, w_big):
    n_slabs = cache_t.shape[0] * cache_t.shape[1]
    x = cache_t.reshape(n_slabs * HEAD_DIM, PAGE_SIZE)
    return pl.pallas_call(
        _paged_compress_body,
        grid=(n_slabs // PC_SLABS,),
        in_specs=[pl.BlockSpec((PC_SLABS * HEAD_DIM, PAGE_SIZE), lambda i: (i, 0)),
                  pl.BlockSpec(w_big.shape, lambda i: (0, 0, 0))],
        out_specs=pl.BlockSpec((PC_SLABS, 4 * HEAD_DIM), lambda i: (i, 0)),
        out_shape=jax.ShapeDtypeStruct((n_slabs, 4 * HEAD_DIM), F32),
        compiler_params=_params(("parallel",)),
        name="paged_compress",
    )(x, w_big)


def _paged_cmp_weight(w):
    eye = jnp.eye(PAGE_SIZE // CMP_BLOCK, dtype=w.dtype)
    wb = jnp.einsum('ide,nm->dnime', w, eye)
    return wb.reshape(HEAD_DIM // 2, 2 * PAGE_SIZE, 4 * HEAD_DIM).astype(BF16)


QROWS = 16


def _head_slopes(shape, axis):
    h = lax.broadcasted_iota(I32, shape, axis)
    s = jnp.zeros(shape, F32)
    for i in range(NSA_HEADS):
        s = jnp.where(h == i, float(2.0 ** (-(i + 1))), s)
    return s


def _nsa_dec_select_body(q_ref, kc_ref, vc_ref, sel_ref, oc_ref, *, past):
    n_c = kc_ref.shape[1]
    half = n_c // 2
    q = q_ref[0]
    lg = lax.dot_general(q, kc_ref[0], NT_DIMS, preferred_element_type=F32)
    col = lax.broadcasted_iota(I32, (QROWS, n_c), 1)
    cblk = 2 * (col % half) + col // half
    dist = (past - (cblk * CMP_BLOCK + (CMP_BLOCK - 1))).astype(F32)
    mask = dist >= 0
    lc = jnp.where(mask, lg - _head_slopes((QROWS, n_c), 0) * dist, NEG_INF)
    m = jnp.max(lc, axis=-1, keepdims=True)
    p = jnp.where(mask, jnp.exp(lc - m), 0.0)
    pc = p / jnp.maximum(jnp.sum(p, axis=-1, keepdims=True), 1e-30)
    o = jnp.dot(pc.astype(BF16), vc_ref[0], preferred_element_type=F32)
    lane_o = lax.broadcasted_iota(I32, (NSA_GROUP, LANES), 1)
    oc_ref[0] = jnp.where(lane_o < HEAD_DIM, o[:NSA_GROUP], o[NSA_GROUP:NSA_HEADS])
    row = lax.broadcasted_iota(I32, (QROWS, n_c), 0)
    lane = lax.broadcasted_iota(I32, (SUBLANES, LANES), 1)
    srow = lax.broadcasted_iota(I32, (SUBLANES, LANES), 0)
    lane_f = lane.astype(F32)
    score = jnp.full((SUBLANES, LANES), -jnp.inf, F32)
    for g in range(NSA_KV_HEADS):
        in_g = (row >= g * NSA_GROUP) & (row < (g + 1) * NSA_GROUP)
        sg = jnp.sum(jnp.where(in_g, pc, 0.0), axis=0, keepdims=True)
        imp = sg[:, :half] + sg[:, half:]
        forced = (lane[0:1] == 0) | (lane[0:1] == past // SLC_BLOCK)
        sc = jnp.where(lane[0:1] * SLC_BLOCK <= past, imp + jnp.where(forced, FORCE_BONUS, 0.0), -1.0)
        score = jnp.where(srow == g, jnp.broadcast_to(sc, (SUBLANES, LANES)), score)
    n_before = jnp.sum(jnp.where(score >= FORCE_BONUS, 1.0, 0.0), axis=-1, keepdims=True)
    new_sel = n_before < float(SLC_TOPK)
    order = jnp.zeros((SUBLANES, LANES), F32)
    s = score
    for k in range(SLC_TOPK):
        mk = jnp.max(s, axis=-1, keepdims=True)
        idx = jnp.min(jnp.where(s == mk, lane_f, float(LANES)), axis=-1, keepdims=True)
        order = jnp.where(lane == k, idx, order)
        s = jnp.where(lane_f == idx, -jnp.inf, s)
    order = jnp.where((lane == SLC_TOPK - 1) & new_sel, float(LANES), order)
    sel_ref[0] = order.astype(I32)


def _nsa_dec_select(qmat, kc, vc, past):
    bd, n_c = kc.shape[0], kc.shape[1]
    return pl.pallas_call(
        functools.partial(_nsa_dec_select_body, past=past),
        grid=(bd,),
        in_specs=[pl.BlockSpec((1, QROWS, LANES), lambda b: (b, 0, 0)),
                  pl.BlockSpec((1, n_c, LANES), lambda b: (b, 0, 0)),
                  pl.BlockSpec((1, n_c, LANES), lambda b: (b, 0, 0))],
        out_specs=[pl.BlockSpec((1, SUBLANES, LANES), lambda b: (b, 0, 0)),
                   pl.BlockSpec((1, NSA_GROUP, LANES), lambda b: (b, 0, 0))],
        out_shape=[jax.ShapeDtypeStruct((bd, SUBLANES, LANES), I32),
                   jax.ShapeDtypeStruct((bd, NSA_GROUP, LANES), F32)],
        compiler_params=_params(("parallel",)),
        name="nsa_decode_select",
    )(qmat, kc, vc)


ND_SLABS = 8
ND_STEPS = SLC_TOPK // ND_SLABS
ND_PAGE, ND_HALF, ND_SLOT, ND_NEW = 0, 32, 64, 96


def _nsa_dec_attend_body(meta_ref, *refs, past):
    ks_refs, vs_refs = refs[:ND_SLABS], refs[ND_SLABS:2 * ND_SLABS]
    (kw_ref, vw_ref, q_ref, nks_ref, nvs_ref, nkw_ref, nvw_ref, o_ref, m_ref, l_ref, acc_ref) = refs[2 * ND_SLABS:]
    bg = pl.program_id(0)
    kh = pl.program_id(1)
    b = bg // NSA_KV_HEADS
    g = bg % NSA_KV_HEADS
    lane = lax.broadcasted_iota(I32, (1, LANES), 1)
    qb = [jnp.broadcast_to(q_ref[0, r], (HEAD_DIM, LANES)) for r in range(NSA_GROUP)]
    slopes = [jnp.where(g == 0, _slope(0, r), _slope(1, r)) for r in range(NSA_GROUP)]

    @pl.when(kh == 0)
    def _():
        m_ref[...] = jnp.full(m_ref.shape, NEG_INF, F32)
        l_ref[...] = jnp.zeros(l_ref.shape, F32)
        acc_ref[...] = jnp.zeros(acc_ref.shape, F32)

    rows = [[] for _ in range(NSA_GROUP)]
    for i in range(ND_SLABS):
        j = g * SLC_TOPK + kh * ND_SLABS + i
        ok = (lane // SLC_BLOCK) == meta_ref[b, ND_HALF + j]
        dist = (past - (meta_ref[b, ND_SLOT + j] * PAGE_SIZE + lane)).astype(F32)
        kt = ks_refs[i][0, 0]
        for r in range(NSA_GROUP):
            rowv = jnp.sum(qb[r] * kt, axis=0, keepdims=True)
            rows[r].append(jnp.where(ok, rowv - slopes[r] * dist, NEG_INF))
    for r in range(NSA_GROUP):
        s = jnp.concatenate(rows[r], axis=0)
        m_old = m_ref[r:r + 1, :]
        m_new = jnp.maximum(m_old, jnp.max(jnp.max(s, axis=-1, keepdims=True), axis=0, keepdims=True))
        alpha = jnp.exp(m_old - m_new)
        p = jnp.where(s > 0.5 * NEG_INF, jnp.exp(s - m_new), 0.0)
        l_ref[r:r + 1, :] = alpha * l_ref[r:r + 1, :] + jnp.sum(jnp.sum(p, axis=-1, keepdims=True), axis=0, keepdims=True)
        acc = alpha * acc_ref[r]
        for i in range(ND_SLABS):
            acc = acc + p[i:i + 1, :] * vs_refs[i][0, 0]
        acc_ref[r] = acc
        m_ref[r:r + 1, :] = m_new

    @pl.when(kh == ND_STEPS - 1)
    def _():
        lane_o = lax.broadcasted_iota(I32, (HEAD_DIM, LANES), 1)
        out = jnp.zeros((HEAD_DIM, LANES), F32)
        new_on = meta_ref[b, ND_NEW + g] > 0
        n_wt = WINDOW // LANES
        wrows = [[] for _ in range(NSA_GROUP)]
        for c in range(n_wt):
            kt = kw_ref[0, :, c * LANES:(c + 1) * LANES]
            dist_i = WINDOW - (c * LANES + lane)
            for r in range(NSA_GROUP):
                rowv = jnp.sum(qb[r] * kt, axis=0, keepdims=True)
                wrows[r].append(jnp.where(dist_i < WINDOW, rowv - slopes[r] * dist_i.astype(F32), NEG_INF))
        for r in range(NSA_GROUP):
            lgn = jnp.where(new_on, jnp.sum(q_ref[0, r] * nks_ref[0], axis=0, keepdims=True), NEG_INF)
            m_old = m_ref[r:r + 1, :]
            m_fin = jnp.maximum(m_old, lgn)
            alpha = jnp.exp(m_old - m_fin)
            pn = jnp.where(new_on, jnp.exp(lgn - m_fin), 0.0)
            den = alpha * l_ref[r:r + 1, :] + pn
            o_col = (alpha * jnp.sum(acc_ref[r], axis=-1, keepdims=True) + pn * nvs_ref[0]) / jnp.maximum(den, 1e-30)
            out = jnp.where(lane_o == 2 * r, jnp.broadcast_to(o_col, (HEAD_DIM, LANES)), out)
            lgw = jnp.sum(q_ref[0, r] * nkw_ref[0], axis=0, keepdims=True)
            lw = jnp.concatenate(wrows[r], axis=0)
            m = jnp.maximum(jnp.max(jnp.max(lw, axis=-1, keepdims=True), axis=0, keepdims=True), lgw)
            p = jnp.where(lw > 0.5 * NEG_INF, jnp.exp(lw - m), 0.0)
            pw = jnp.exp(lgw - m)
            den = jnp.sum(jnp.sum(p, axis=-1, keepdims=True), axis=0, keepdims=True) + pw
            acc = jnp.zeros((HEAD_DIM, LANES), F32)
            for c in range(n_wt):
                acc = acc + p[c:c + 1, :] * vw_ref[0, :, c * LANES:(c + 1) * LANES]
            o_col = (jnp.sum(acc, axis=-1, keepdims=True) + pw * nvw_ref[0]) / jnp.maximum(den, 1e-30)
            out = jnp.where(lane_o == 2 * r + 1, jnp.broadcast_to(o_col, (HEAD_DIM, LANES)), out)
        o_ref[0] = out


def _nsa_dec_meta(page_table, sel, past):
    bd = sel.shape[0]
    n_past_blk = past // SLC_BLOCK
    is_new = sel >= n_past_blk
    sp = jnp.minimum(sel, n_past_blk - 1)
    slot = sp // (PAGE_SIZE // SLC_BLOCK)
    page = jnp.take_along_axis(page_table, slot, axis=1)
    half = jnp.where(is_new, 2, sp % (PAGE_SIZE // SLC_BLOCK))
    has_new = jnp.any(is_new.reshape(bd, NSA_KV_HEADS, SLC_TOPK), axis=-1).astype(I32)
    pad = jnp.zeros((bd, LANES - ND_NEW - NSA_KV_HEADS), I32)
    return jnp.concatenate([page, half, slot, has_new, pad], axis=1).astype(I32)


def _nsa_dec_attend(meta, slc_k_t, slc_v_t, win_k_t, win_v_t, qcol, nks, nvs, nkw, nvw, past):
    n_bg = qcol.shape[0]

    def slab_spec(i):
        def imap(bg, kh, mt):
            g = bg % NSA_KV_HEADS
            return (mt[bg // NSA_KV_HEADS, ND_PAGE + g * SLC_TOPK + kh * ND_SLABS + i], g, 0, 0)
        return pl.BlockSpec((1, 1, HEAD_DIM, PAGE_SIZE), imap)

    slabs = [slab_spec(i) for i in range(ND_SLABS)]
    wsp = pl.BlockSpec((1, HEAD_DIM, WINDOW), lambda bg, kh, mt: (bg, 0, 0))
    csp = pl.BlockSpec((1, HEAD_DIM, 1), lambda bg, kh, mt: (bg, 0, 0))
    grid_spec = pltpu.PrefetchScalarGridSpec(
        num_scalar_prefetch=1,
        grid=(n_bg, ND_STEPS),
        in_specs=slabs + slabs + [wsp, wsp, pl.BlockSpec((1, NSA_GROUP, HEAD_DIM, 1), lambda bg, kh, mt: (bg, 0, 0, 0)),
                                  csp, csp, csp, csp],
        out_specs=pl.BlockSpec((1, HEAD_DIM, LANES), lambda bg, kh, mt: (bg, 0, 0)),
        scratch_shapes=[pltpu.VMEM((NSA_GROUP, 1), F32),
                        pltpu.VMEM((NSA_GROUP, 1), F32),
                        pltpu.VMEM((NSA_GROUP, HEAD_DIM, LANES), F32)],
    )
    return pl.pallas_call(
        functools.partial(_nsa_dec_attend_body, past=past),
        grid_spec=grid_spec,
        out_shape=jax.ShapeDtypeStruct((n_bg, HEAD_DIM, LANES), F32),
        compiler_params=_params(("parallel", "arbitrary")),
        name="nsa_decode_attend",
    )(meta, *([slc_k_t] * ND_SLABS), *([slc_v_t] * ND_SLABS), win_k_t, win_v_t, qcol, nks, nvs, nkw, nvw)


FD_PPS = 8


def _fox_dec_body(pt_ref, *refs):
    del pt_ref
    k_refs, v_refs, lf_refs = refs[:FD_PPS], refs[FD_PPS:2 * FD_PPS], refs[2 * FD_PPS:3 * FD_PPS]
    q_ref, nk_ref, nv_ref, nlf_ref, o_ref, qb_ref, m_ref, l_ref, acc_ref, carry_ref = refs[3 * FD_PPS:]
    j = pl.program_id(1)
    nh = FOX_HEADS
    lane3 = lax.broadcasted_iota(I32, (nh, HEAD_DIM, LANES), 2)

    @pl.when(j == 0)
    def _():
        q = q_ref[0]
        qb_ref[...] = jnp.broadcast_to(q, (nh, HEAD_DIM, LANES))
        m_ref[...] = jnp.sum(q * nk_ref[0], axis=1)
        l_ref[...] = jnp.ones(l_ref.shape, F32)
        acc_ref[...] = jnp.where(lane3 == 0, jnp.broadcast_to(nv_ref[0], (nh, HEAD_DIM, LANES)), 0.0)
        carry_ref[...] = nlf_ref[0]

    r = lax.broadcasted_iota(I32, (PAGE_SIZE, PAGE_SIZE), 0)
    c = lax.broadcasted_iota(I32, (PAGE_SIZE, PAGE_SIZE), 1)
    later = jnp.where(r > c, 1.0, 0.0).astype(BF16)
    for i in range(FD_PPS):
        lf = lf_refs[i][0, 0]
        h3, m3, l3 = _split3(lf)
        sfx = (jnp.dot(h3, later, preferred_element_type=F32) + jnp.dot(m3, later, preferred_element_type=F32)
               + jnp.dot(l3, later, preferred_element_type=F32))
        bias = carry_ref[...] + sfx
        carry_ref[...] = carry_ref[...] + jnp.sum(lf, axis=-1, keepdims=True)
        rows = [jnp.sum(qb_ref[h] * k_refs[i][0, 0, h], axis=0, keepdims=True) for h in range(nh)]
        s = jnp.concatenate(rows, axis=0) + bias
        m_old = m_ref[...]
        m_new = jnp.maximum(m_old, jnp.max(s, axis=-1, keepdims=True))
        alpha = jnp.exp(m_old - m_new)
        p = jnp.exp(s - m_new)
        l_ref[...] = alpha * l_ref[...] + jnp.sum(p, axis=-1, keepdims=True)
        m_ref[...] = m_new
        for h in range(nh):
            acc_ref[h] = alpha[h:h + 1, :] * acc_ref[h] + p[h:h + 1, :] * v_refs[i][0, 0, h]

    @pl.when(j == pl.num_programs(1) - 1)
    def _():
        lane_o = lax.broadcasted_iota(I32, (HEAD_DIM, LANES), 1)
        out = jnp.zeros((HEAD_DIM, LANES), F32)
        l = l_ref[...]
        for h in range(nh):
            o_col = jnp.sum(acc_ref[h], axis=-1, keepdims=True) / l[h:h + 1, :]
            out = jnp.where(lane_o == h, jnp.broadcast_to(o_col, (HEAD_DIM, LANES)), out)
        o_ref[0] = out


def _fox_decode(page_table, fox_k_t, fox_v_t, fox_lf_t, qcol, nk, nv, nlf):
    bd, n_pages = page_table.shape
    n_steps = n_pages // FD_PPS

    def page(i):
        return lambda b, j, pt: (0, pt[b, n_pages - 1 - (j * FD_PPS + i)], 0, 0, 0)

    def page4(i):
        return lambda b, j, pt: (0, pt[b, n_pages - 1 - (j * FD_PPS + i)], 0, 0)

    kv_specs = [pl.BlockSpec((1, 1, FOX_HEADS, HEAD_DIM, PAGE_SIZE), page(i)) for i in range(FD_PPS)]
    lf_specs = [pl.BlockSpec((1, 1, FOX_HEADS, PAGE_SIZE), page4(i)) for i in range(FD_PPS)]
    csp = pl.BlockSpec((1, FOX_HEADS, HEAD_DIM, 1), lambda b, j, pt: (b, 0, 0, 0))
    grid_spec = pltpu.PrefetchScalarGridSpec(
        num_scalar_prefetch=1,
        grid=(bd, n_steps),
        in_specs=kv_specs + kv_specs + lf_specs + [csp, csp, csp, pl.BlockSpec((1, FOX_HEADS, 1), lambda b, j, pt: (b, 0, 0))],
        out_specs=pl.BlockSpec((1, HEAD_DIM, LANES), lambda b, j, pt: (b, 0, 0)),
        scratch_shapes=[pltpu.VMEM((FOX_HEADS, HEAD_DIM, LANES), F32),
                        pltpu.VMEM((FOX_HEADS, 1), F32),
                        pltpu.VMEM((FOX_HEADS, 1), F32),
                        pltpu.VMEM((FOX_HEADS, HEAD_DIM, LANES), F32),
                        pltpu.VMEM((FOX_HEADS, 1), F32)],
    )
    return pl.pallas_call(
        _fox_dec_body,
        grid_spec=grid_spec,
        out_shape=jax.ShapeDtypeStruct((bd, HEAD_DIM, LANES), F32),
        compiler_params=_params(("parallel", "arbitrary")),
        name="fox_decode",
    )(page_table, *([fox_k_t] * FD_PPS), *([fox_v_t] * FD_PPS), *([fox_lf_t] * FD_PPS), qcol, nk, nv, nlf)


def _outproj_dec_body(oc_ref, os_ref, ow_ref, gc_ref, gs_ref, gw_ref, of_ref, x_ref, g0_ref, b0_ref,
                      wn_ref, wf_ref, g1_ref, b1_ref, y_ref):
    xn = _layer_norm(x_ref[...], g0_ref[...], b0_ref[...])
    o_n = gc_ref[...] * oc_ref[...] + gs_ref[...] * os_ref[...] + gw_ref[...] * ow_ref[...]
    a = (jnp.dot(o_n.astype(BF16), wn_ref[...], preferred_element_type=F32)
         + jnp.dot(of_ref[...].astype(BF16), wf_ref[...], preferred_element_type=F32))
    y_ref[...] = _layer_norm(DN_ALPHA * xn + a, g1_ref[...], b1_ref[...])


def _outproj_dec(oc, osl, ow, gc, gs, gw, o_f, x2, g0, b0, w_n, w_f, g1, b1):
    n = x2.shape[0]
    args = (oc, osl, ow, gc, gs, gw, o_f, x2, g0, b0, w_n, w_f, g1, b1)
    return pl.pallas_call(
        _outproj_dec_body,
        grid=(1,),
        in_specs=[pl.BlockSpec(a.shape, lambda i: (0,) * a.ndim) for a in args],
        out_specs=pl.BlockSpec((n, D_MODEL), lambda i: (0, 0)),
        out_shape=jax.ShapeDtypeStruct((n, D_MODEL), F32),
        compiler_params=_params(("arbitrary",)),
        name="outproj_ln1_decode",
    )(*args)


def kernel(x_prompt, x_sample, cache_fox_k, cache_fox_v, cache_fox_logf, cache_cmp_k, cache_cmp_v, cache_slc_k, cache_slc_v, state_win_k, state_win_v, page_table, ln_in_g, ln_in_b, w_in, b_in, w_cmp_k, w_cmp_v, w_out, ln1_g, ln1_b, w_router, b_router, w_exp_gate, w_exp_up, w_exp_down, w_sh_gate, w_sh_up, w_sh_down, ln2_g, ln2_b):
    batch, seq, _ = x_prompt.shape
    bd = x_sample.shape[0]
    l = 0
    g0, b0 = ln_in_g.reshape(1, D_MODEL), ln_in_b.reshape(1, D_MODEL)
    wp, bp = _prep_proj_weights(w_in[l], b_in[l])
    wck, wcv = _blockdiag_cmp_weight(w_cmp_k[l]), _blockdiag_cmp_weight(w_cmp_v[l])
    nsa_perm = jnp.asarray(_nsa_col_perm())
    w_on = jnp.take(w_out[l][:NSA_HEADS * HEAD_DIM], nsa_perm, axis=0).astype(BF16)
    w_of = w_out[l][NSA_HEADS * HEAD_DIM:].astype(BF16)
    wr_t = w_router[l].T.astype(BF16)
    br_col = b_router[l].reshape(N_EXPERTS, 1)
    wsg, wsu, wsd = w_sh_gate[l].astype(BF16), w_sh_up[l].astype(BF16), w_sh_down[l].astype(BF16)
    g1, b1, g2, b2 = ln1_g[l:l + 1], ln1_b[l:l + 1], ln2_g[l:l + 1], ln2_b[l:l + 1]

    xp2 = x_prompt.reshape(batch * seq, D_MODEL)
    (qn, qf, kfb, vfb, ksb, vsb, kwb, vwb, fk, fv, ck, cv, sk, sv, wk, wv, gl) = _project(xp2, g0, b0, wp, bp, 512)
    kc, vc = _compress_prompt(ck, cv, wck, wcv, batch, seq)
    o_n = _nsa_prompt(qn, gl, kc, vc, ksb, vsb, kwb, vwb, batch, seq)
    ccol, crow = _logf_cumsum(gl, batch, seq)
    o_f = _fox_prompt(qf, kfb, vfb, ccol, crow, batch, seq)
    x1 = _outproj_ln1(o_n, o_f, xp2, g0, b0, w_on, w_of, g1, b1, 512)
    y_p = _moe_ln2(x1, wr_t, br_col, w_exp_gate[l], w_exp_up[l], w_exp_down[l], wsg, wsu, wsd, g2, b2, 256, 256)

    st5 = lambda a, h: a.reshape(1, batch, seq, h, HEAD_DIM)
    keep_p = min(WINDOW, seq)
    p_outs = (st5(fk, FOX_HEADS), st5(fv, FOX_HEADS),
              gl[:, GL_LOGF:GL_LOGF + FOX_HEADS].reshape(1, batch, seq, FOX_HEADS),
              st5(ck, NSA_KV_HEADS), st5(cv, NSA_KV_HEADS), st5(sk, NSA_KV_HEADS), st5(sv, NSA_KV_HEADS),
              st5(wk, NSA_KV_HEADS)[:, :, seq - keep_p:], st5(wv, NSA_KV_HEADS)[:, :, seq - keep_p:])
    n_pages = page_table.shape[1]
    n_phys = cache_cmp_k.shape[1]
    past = n_pages * PAGE_SIZE
    assert x_sample.shape[1] == 1 and past // SLC_BLOCK == LANES and state_win_k.shape[2] == WINDOW
    xs2 = x_sample.reshape(bd, D_MODEL)
    (qn_s, qf_s, _, _, _, _, _, _, fk_s, fv_s, ck_s, cv_s, sk_s, sv_s, wk_s, wv_s, gl_s) = _project(xs2, g0, b0, wp, bp, bd)

    page_minor = lambda c: jnp.transpose(c[l], (0, 2, 3, 1))
    kc_phys = _paged_compress(page_minor(cache_cmp_k), _paged_cmp_weight(w_cmp_k[l]))
    vc_phys = _paged_compress(page_minor(cache_cmp_v), _paged_cmp_weight(w_cmp_v[l]))

    def seq_blocks(c_phys):
        blocks_per_page = PAGE_SIZE // CMP_BLOCK
        c = c_phys.reshape(n_phys, NSA_KV_HEADS, blocks_per_page, HEAD_DIM)[page_table]
        c = c.transpose(0, 1, 3, 2, 4).reshape(bd, n_pages * blocks_per_page // 2, 2, 128)
        return c.transpose(0, 2, 1, 3).reshape(bd, n_pages * blocks_per_page, 128).astype(BF16)

    q4 = qn_s.reshape(bd, NSA_GROUP, NSA_KV_HEADS, HEAD_DIM).transpose(0, 2, 1, 3)
    zq = jnp.zeros_like(q4[:, 0])
    qmat = jnp.concatenate([jnp.concatenate([q4[:, 0], zq], axis=-1), jnp.concatenate([zq, q4[:, 1]], axis=-1)], axis=1)
    qmat = jnp.pad(qmat, ((0, 0), (0, QROWS - NSA_HEADS), (0, 0)))
    sel, o_cmp = _nsa_dec_select(qmat, seq_blocks(kc_phys), seq_blocks(vc_phys), past)
    sel2 = sel[:, :NSA_KV_HEADS, :SLC_TOPK].reshape(bd, NSA_KV_HEADS * SLC_TOPK)
    col = lambda a, h: a.reshape(bd, h, HEAD_DIM, 1)
    gcol = lambda a: a.reshape(bd * NSA_KV_HEADS, HEAD_DIM, 1)
    win_minor = lambda c: jnp.transpose(c[l], (0, 2, 3, 1)).reshape(bd * NSA_KV_HEADS, HEAD_DIM, WINDOW)
    ocols = _nsa_dec_attend(_nsa_dec_meta(page_table, sel2, past), page_minor(cache_slc_k), page_minor(cache_slc_v),
                            win_minor(state_win_k), win_minor(state_win_v),
                            q4.astype(F32).reshape(bd * NSA_KV_HEADS, NSA_GROUP, HEAD_DIM, 1),
                            gcol(sk_s), gcol(sv_s), gcol(wk_s), gcol(wv_s), past)
    ocols = ocols.reshape(bd, NSA_KV_HEADS, HEAD_DIM, LANES)
    to_cols = lambda a: a.reshape(bd, NSA_KV_HEADS, NSA_GROUP, HEAD_DIM).transpose(0, 2, 1, 3).reshape(bd, NSA_HEADS * HEAD_DIM)
    oc = o_cmp.reshape(bd, NSA_HEADS * HEAD_DIM)
    osl = to_cols(ocols[..., 0:2 * NSA_GROUP:2].transpose(0, 1, 3, 2))
    ow = to_cols(ocols[..., 1:2 * NSA_GROUP:2].transpose(0, 1, 3, 2))
    gates = gl_s[:, :GL_LOGF].reshape(bd, NSA_KV_HEADS, NSA_GROUP, N_BRANCH)
    gexp = lambda br: to_cols(jnp.broadcast_to(gates[..., br:br + 1], (bd, NSA_KV_HEADS, NSA_GROUP, HEAD_DIM)))

    fox_minor = lambda c: jnp.transpose(c, (0, 1, 3, 4, 2))
    of_cols = _fox_decode(page_table, fox_minor(cache_fox_k), fox_minor(cache_fox_v),
                          jnp.transpose(cache_fox_logf, (0, 1, 3, 2)),
                          col(qf_s.astype(F32), FOX_HEADS), col(fk_s, FOX_HEADS), col(fv_s, FOX_HEADS),
                          gl_s[:, GL_LOGF:GL_LOGF + FOX_HEADS].reshape(bd, FOX_HEADS, 1))
    of_s = of_cols[:, :, :FOX_HEADS].transpose(0, 2, 1).reshape(bd, FOX_HEADS * HEAD_DIM)
    x1_s = _outproj_dec(oc, osl, ow, gexp(0), gexp(1), gexp(2), of_s, xs2, g0, b0, w_on, w_of, g1, b1)
    y_s = _moe_ln2(x1_s, wr_t, br_col, w_exp_gate[l], w_exp_up[l], w_exp_down[l], wsg, wsu, wsd, g2, b2, 128, 128)

    ss5 = lambda a, h: a.reshape(1, bd, 1, h, HEAD_DIM)
    keep_s = min(WINDOW, past + 1)
    win_new = lambda buf, new: jnp.concatenate([buf, ss5(new, NSA_KV_HEADS)], axis=2)[:, :, buf.shape[2] + 1 - keep_s:]
    s_outs = (ss5(fk_s, FOX_HEADS), ss5(fv_s, FOX_HEADS),
              gl_s[:, GL_LOGF:GL_LOGF + FOX_HEADS].reshape(1, bd, 1, FOX_HEADS),
              ss5(ck_s, NSA_KV_HEADS), ss5(cv_s, NSA_KV_HEADS), ss5(sk_s, NSA_KV_HEADS), ss5(sv_s, NSA_KV_HEADS),
              win_new(state_win_k, wk_s), win_new(state_win_v, wv_s))
    return (y_p.reshape(batch, seq, D_MODEL), y_s.reshape(bd, 1, D_MODEL)) + p_outs + s_outs
```

```python
import functools

import numpy as np
import jax
import jax.numpy as jnp
from jax import lax
from jax.experimental import pallas as pl
from jax.experimental.pallas import tpu as pltpu

F32 = jnp.float32
BF16 = jnp.bfloat16
I32 = jnp.int32

D_MODEL = 1024
HEAD_DIM = 64
NSA_HEADS = 8
NSA_KV_HEADS = 2
NSA_GROUP = NSA_HEADS // NSA_KV_HEADS
FOX_HEADS = 8
CMP_BLOCK = 32
SLC_BLOCK = 64
SLC_TOPK = 16
WINDOW = 512
N_BRANCH = 3
FORCE_BONUS = 1000.0
N_EXPERTS = 64
N_EXPERT_GROUPS = 8
EXPERTS_PER_GROUP = N_EXPERTS // N_EXPERT_GROUPS
TOPK_GROUPS = 4
TOP_K = 4
D_EXPERT = 256
D_SHARED = 256
ROUTED_SCALE = 2.5
LN_EPS = 1e-5
DEPTH = 1
DN_ALPHA = (2.0 * DEPTH) ** 0.25
NEG_INF = -1e30
PAGE_SIZE = 128

LANES = 128
SUBLANES = 8
VMEM_LIMIT = 56 * 1024 * 1024

C_QN, C_QF, C_KF, C_VF = 0, 512, 1024, 1536
C_KC, C_VC, C_KS, C_VS, C_KW, C_VW, C_GL = 2048, 2176, 2304, 2432, 2560, 2688, 2816
N_PROJ = 2944
GL_LOGF = NSA_HEADS * N_BRANCH

R_QN, R_KC, R_VC, R_KS, R_VS, R_KW, R_VW, R_GN, R_QF, R_KF, R_VF, R_FF = (
    0, 512, 640, 768, 896, 1024, 1152, 1280, 1304, 1816, 2328, 2840)

NT_DIMS = (((1,), (1,)), ((), ()))


def _slope(g, r):
    return float(2.0 ** (-(g * NSA_GROUP + r + 1)))


def _nsa_col_perm():
    perm = np.zeros(NSA_HEADS * HEAD_DIM, np.int32)
    for j in range(NSA_GROUP):
        for g in range(NSA_KV_HEADS):
            for d in range(HEAD_DIM):
                perm[j * 128 + g * 64 + d] = (g * NSA_GROUP + j) * HEAD_DIM + d
    return perm


def _proj_perm():
    perm = np.full(N_PROJ, -1, np.int64)
    scale = np.ones(N_PROJ, np.float32)
    perm[C_QN:C_QN + 512] = R_QN + _nsa_col_perm()
    scale[C_QN:C_QN + 512] = HEAD_DIM ** -0.5
    perm[C_QF:C_QF + 512] = R_QF + np.arange(512)
    scale[C_QF:C_QF + 512] = HEAD_DIM ** -0.5
    perm[C_KF:C_KF + 512] = R_KF + np.arange(512)
    perm[C_VF:C_VF + 512] = R_VF + np.arange(512)
    for c, r in ((C_KC, R_KC), (C_VC, R_VC), (C_KS, R_KS), (C_VS, R_VS), (C_KW, R_KW), (C_VW, R_VW)):
        perm[c:c + 128] = r + np.arange(128)
    perm[C_GL:C_GL + 24] = R_GN + np.arange(24)
    perm[C_GL + 24:C_GL + 32] = R_FF + np.arange(8)
    return perm, scale


def _prep_proj_weights(w_in, b_in):
    perm, scale = _proj_perm()
    w_parts, b_parts = [], []
    c = 0
    while c < N_PROJ:
        e = c + 1
        while e < N_PROJ and (perm[e] == perm[e - 1] + 1 if perm[c] >= 0 else perm[e] < 0) and scale[e] == scale[c]:
            e += 1
        if perm[c] >= 0:
            w_parts.append(w_in[:, perm[c]:perm[c] + e - c] * float(scale[c]))
            b_parts.append(b_in[perm[c]:perm[c] + e - c] * float(scale[c]))
        else:
            w_parts.append(jnp.zeros((w_in.shape[0], e - c), w_in.dtype))
            b_parts.append(jnp.zeros((e - c,), b_in.dtype))
        c = e
    wp = jnp.concatenate(w_parts, axis=1).astype(BF16)
    bp = jnp.concatenate(b_parts).reshape(1, N_PROJ)
    return wp, bp


def _layer_norm(x, g, b):
    mu = jnp.mean(x, axis=-1, keepdims=True)
    xc = x - mu
    var = jnp.mean(xc * xc, axis=-1, keepdims=True)
    return xc * lax.rsqrt(var + LN_EPS) * g + b


def _split3(x):
    h = x.astype(BF16)
    r = x - h.astype(F32)
    m = r.astype(BF16)
    l = (r - m.astype(F32)).astype(BF16)
    return h, m, l


def _params(sem):
    return pltpu.CompilerParams(dimension_semantics=sem, vmem_limit_bytes=VMEM_LIMIT)


def _proj_body(x_ref, g_ref, b_ref, w_ref, bias_ref,
               qn_ref, qf_ref, kfb_ref, vfb_ref, ksb_ref, vsb_ref, kwb_ref, vwb_ref,
               fk_ref, fv_ref, ck_ref, cv_ref, sk_ref, sv_ref, wk_ref, wv_ref, gl_ref):
    xn = _layer_norm(x_ref[...], g_ref[...], b_ref[...])
    xb = xn.astype(BF16)

    def sec(c, n):
        return jnp.dot(xb, w_ref[:, c:c + n], preferred_element_type=F32) + bias_ref[:, c:c + n]

    qn_ref[...] = sec(C_QN, 512).astype(BF16)
    qf_ref[...] = sec(C_QF, 512).astype(BF16)
    h = sec(C_KF, 512)
    fk_ref[...] = h
    kfb_ref[...] = h.astype(BF16)
    h = sec(C_VF, 512)
    fv_ref[...] = h
    vfb_ref[...] = h.astype(BF16)
    ck_ref[...] = sec(C_KC, 128)
    cv_ref[...] = sec(C_VC, 128)
    for c, f_ref, b_ref2 in ((C_KS, sk_ref, ksb_ref), (C_VS, sv_ref, vsb_ref),
                             (C_KW, wk_ref, kwb_ref), (C_VW, wv_ref, vwb_ref)):
        h = sec(c, 128)
        f_ref[...] = h
        b_ref2[...] = h.astype(BF16)
    h = sec(C_GL, 128)
    lane = lax.broadcasted_iota(I32, h.shape, 1)
    gl_ref[...] = jnp.where(lane < GL_LOGF, jax.nn.sigmoid(h), jax.nn.log_sigmoid(h))


def _project(x2, ln_g, ln_b, wp, bp, tm):
    n = x2.shape[0]
    bf = lambda w: jax.ShapeDtypeStruct((n, w), BF16)
    ff = lambda w: jax.ShapeDtypeStruct((n, w), F32)
    out_shape = [bf(512), bf(512), bf(512), bf(512), bf(128), bf(128), bf(128), bf(128),
                 ff(512), ff(512), ff(128), ff(128), ff(128), ff(128), ff(128), ff(128), ff(128)]
    row = lambda w: pl.BlockSpec((tm, w), lambda i: (i, 0))
    full = lambda a: pl.BlockSpec(a.shape, lambda i: (0,) * a.ndim)
    return pl.pallas_call(
        _proj_body,
        grid=(n // tm,),
        in_specs=[row(D_MODEL), full(ln_g), full(ln_b), full(wp), full(bp)],
        out_specs=[row(s.shape[1]) for s in out_shape],
        out_shape=out_shape,
        compiler_params=_params(("parallel",)),
        name="ln_in_proj",
    )(x2, ln_g, ln_b, wp, bp)


def _compress_body(xk_ref, xv_ref, wk_ref, wv_ref, kc_ref, vc_ref):
    half = kc_ref.shape[1] // 2
    for x_ref, w_ref, o_ref in ((xk_ref, wk_ref, kc_ref), (xv_ref, wv_ref, vc_ref)):
        for par in range(2):
            acc = jnp.zeros((half, 128), F32)
            for i in range(CMP_BLOCK):
                rows = x_ref[pl.ds(par * CMP_BLOCK + i, half, stride=2 * CMP_BLOCK), :]
                acc = acc + jnp.dot(rows.astype(BF16), w_ref[i], preferred_element_type=F32)
            o_ref[0, par * half:(par + 1) * half, :] = acc.astype(BF16)


def _compress_prompt(ck, cv, wck, wcv, batch, seq):
    n_c = seq // CMP_BLOCK
    rows = pl.BlockSpec((seq, 128), lambda b: (b, 0))
    wsp = pl.BlockSpec((CMP_BLOCK, 128, 128), lambda b: (0, 0, 0))
    osp = pl.BlockSpec((1, n_c, 128), lambda b: (b, 0, 0))
    return pl.pallas_call(
        _compress_body,
        grid=(batch,),
        in_specs=[rows, rows, wsp, wsp],
        out_specs=[osp, osp],
        out_shape=[jax.ShapeDtypeStruct((batch, n_c, 128), BF16)] * 2,
        compiler_params=_params(("parallel",)),
        name="compress_prompt",
    )(ck, cv, wck, wcv)


def _blockdiag_cmp_weight(w):
    z = jnp.zeros_like(w)
    top = jnp.concatenate([w, z], axis=2)
    bot = jnp.concatenate([z, w], axis=2)
    return jnp.concatenate([top, bot], axis=1).astype(BF16)


NSA_TQ = 128
NSA_TK = 512


def _top16_mask(score, lane_f):
    sel = jnp.zeros(score.shape, F32)
    s = score
    for _ in range(SLC_TOPK):
        m = jnp.max(s, axis=-1, keepdims=True)
        idx = jnp.min(jnp.where(s == m, lane_f, float(LANES)), axis=-1, keepdims=True)
        hit = lane_f == idx
        sel = jnp.where(hit, 1.0, sel)
        s = jnp.where(hit, -jnp.inf, s)
    return sel


def _nsa_prompt_body(q_ref, gl_ref, kc_ref, vc_ref, ks_ref, vs_ref, kw_ref, vw_ref, o_ref,
                     qs_ref, m_ref, l_ref, acc_ref, flags_ref):
    i = pl.program_id(1)
    q0 = i * NSA_TQ
    n_c = kc_ref.shape[1]
    lane = lax.broadcasted_iota(I32, (NSA_TQ, LANES), 1)
    row = lax.broadcasted_iota(I32, (NSA_TQ, LANES), 0)
    lo = lane < HEAD_DIM
    zero_b = jnp.zeros((NSA_TQ, LANES), BF16)
    for j in range(NSA_GROUP):
        qj = q_ref[:, j * 128:(j + 1) * 128]
        qs_ref[2 * j] = jnp.where(lo, qj, zero_b)
        qs_ref[2 * j + 1] = jnp.where(lo, zero_b, qj)
    qs = qs_ref[...].reshape(8 * NSA_TQ, LANES)
    slopes = [_slope(rb % 2, rb // 2) for rb in range(8)]

    lc_all = lax.dot_general(qs, kc_ref[0], NT_DIMS, preferred_element_type=F32)
    col = lax.broadcasted_iota(I32, (NSA_TQ, n_c), 1)
    half = n_c // 2
    cblk = 2 * (col % half) + col // half
    pos_c = q0 + lax.broadcasted_iota(I32, (NSA_TQ, n_c), 0)
    dist_c = (pos_c - (cblk * CMP_BLOCK + (CMP_BLOCK - 1))).astype(F32)
    mask_c = dist_c >= 0
    psum = [jnp.zeros((NSA_TQ, n_c), F32), jnp.zeros((NSA_TQ, n_c), F32)]
    pcs = []
    for rb in range(8):
        lc = lc_all[rb * NSA_TQ:(rb + 1) * NSA_TQ] - slopes[rb] * dist_c
        lc = jnp.where(mask_c, lc, NEG_INF)
        m = jnp.max(lc, axis=-1, keepdims=True)
        p = jnp.where(mask_c, jnp.exp(lc - m), 0.0)
        pc = p / jnp.maximum(jnp.sum(p, axis=-1, keepdims=True), 1e-30)
        psum[rb % 2] = psum[rb % 2] + pc
        pcs.append(pc.astype(BF16))
    o_cmp = jnp.dot(jnp.concatenate(pcs, axis=0), vc_ref[0], preferred_element_type=F32)

    lane_f = lane.astype(F32)
    pos = q0 + row
    forced = (lane == 0) | (lane == pos // SLC_BLOCK)
    visible = lane * SLC_BLOCK <= pos
    sels = []
    for g in range(NSA_KV_HEADS):
        imp = psum[g][:, :half] + psum[g][:, half:]
        score = jnp.where(visible, imp + jnp.where(forced, FORCE_BONUS, 0.0), -1.0)
        sels.append(_top16_mask(score, lane_f))
    selstack = jnp.concatenate(sels, axis=0).astype(BF16)

    m_ref[...] = jnp.full(m_ref.shape, NEG_INF, F32)
    l_ref[...] = jnp.zeros(l_ref.shape, F32)
    acc_ref[...] = jnp.zeros(acc_ref.shape, F32)
    n_tiles = (q0 + NSA_TQ + NSA_TK - 1) // NSA_TK
    blk_per_tile = NSA_TK // SLC_BLOCK
    n_chunks = NSA_TK // LANES
    blockany = jnp.max(jnp.maximum(sels[0], sels[1]), axis=0, keepdims=True)
    tile_of_blk = lax.broadcasted_iota(I32, (1, LANES), 1) // blk_per_tile
    for t in range(LANES // blk_per_tile):
        flags_ref[t] = (jnp.max(jnp.where(tile_of_blk == t, blockany, 0.0)) > 0.5).astype(I32)
    e_row = lax.broadcasted_iota(I32, (LANES, NSA_TK), 0)
    e_col = lax.broadcasted_iota(I32, (LANES, NSA_TK), 1) // SLC_BLOCK
    d0 = row - lane

    def slc_tile(t, causal):
        k0 = pl.multiple_of(t * NSA_TK, NSA_TK)
        kt = ks_ref[0, pl.ds(k0, NSA_TK), :]
        vt = vs_ref[0, pl.ds(k0, NSA_TK), :]
        s_all = lax.dot_general(qs, kt, NT_DIMS, preferred_element_type=F32)
        expand = jnp.where(e_row == e_col + t * blk_per_tile, 1.0, 0.0).astype(BF16)
        mexp = jnp.dot(selstack, expand, preferred_element_type=F32)
        dist_i = [d0 + (q0 - k0 - c * LANES) for c in range(n_chunks)]
        dist = [d.astype(F32) for d in dist_i]
        oks = []
        for g in range(NSA_KV_HEADS):
            mg = [mexp[g * NSA_TQ:(g + 1) * NSA_TQ, c * LANES:(c + 1) * LANES] for c in range(n_chunks)]
            if causal:
                mg = [jnp.where(dist_i[c] >= 0, mg[c], 0.0) for c in range(n_chunks)]
            oks.append([m > 0.5 for m in mg])
        for rb in range(8):
            rs = slice(rb * NSA_TQ, (rb + 1) * NSA_TQ)
            cols = [jnp.where(oks[rb % 2][c], s_all[rs, c * LANES:(c + 1) * LANES] - slopes[rb] * dist[c], NEG_INF)
                    for c in range(n_chunks)]
            mc = cols[0]
            for c in range(1, n_chunks):
                mc = jnp.maximum(mc, cols[c])
            m_old = m_ref[rs]
            m_new = jnp.maximum(m_old, jnp.max(mc, axis=-1, keepdims=True))
            alpha = jnp.exp(m_old - m_new)
            ps = [jnp.exp(col - m_new) for col in cols]
            psum_l = ps[0]
            for c in range(1, n_chunks):
                psum_l = psum_l + ps[c]
            l_ref[rs] = alpha * l_ref[rs] + psum_l
            pb = jnp.concatenate([p.astype(BF16) for p in ps], axis=1)
            acc_ref[rs] = alpha * acc_ref[rs] + jnp.dot(pb, vt, preferred_element_type=F32)
            m_ref[rs] = m_new

    def maybe_tile(t, carry):
        @pl.when(flags_ref[t] > 0)
        def _():
            slc_tile(t, False)
        return carry

    lax.fori_loop(0, n_tiles - 1, maybe_tile, 0)
    slc_tile(n_tiles - 1, True)

    tw = WINDOW + NSA_TQ
    ws = pl.multiple_of(jnp.maximum(q0 - WINDOW, 0), NSA_TQ)
    kwt = kw_ref[0, pl.ds(ws, tw), :]
    vwt = vw_ref[0, pl.ds(ws, tw), :]
    sw_all = lax.dot_general(qs, kwt, NT_DIMS, preferred_element_type=F32)
    dist_wi = (q0 + lax.broadcasted_iota(I32, (NSA_TQ, tw), 0)) - (ws + lax.broadcasted_iota(I32, (NSA_TQ, tw), 1))
    dist_w = dist_wi.astype(F32)
    ok_w = jnp.where(dist_wi >= 0, jnp.where(dist_wi < WINDOW, 1.0, 0.0), 0.0) > 0.5
    pws = []
    for rb in range(8):
        s = jnp.where(ok_w, sw_all[rb * NSA_TQ:(rb + 1) * NSA_TQ] - slopes[rb] * dist_w, NEG_INF)
        m = jnp.max(s, axis=-1, keepdims=True)
        p = jnp.where(ok_w, jnp.exp(s - m), 0.0)
        pw = p / jnp.maximum(jnp.sum(p, axis=-1, keepdims=True), 1e-30)
        pws.append(pw.astype(BF16))
    o_win = jnp.dot(jnp.concatenate(pws, axis=0), vwt, preferred_element_type=F32)

    gl = gl_ref[...]
    outs = []
    for rb in range(8):
        g, r = rb % 2, rb // 2
        c0 = g * NSA_GROUP * N_BRANCH + r * N_BRANCH
        rs = slice(rb * NSA_TQ, (rb + 1) * NSA_TQ)
        o_slc = acc_ref[rs] / jnp.maximum(jnp.sum(l_ref[rs], axis=-1, keepdims=True), 1e-30)
        outs.append(gl[:, c0:c0 + 1] * o_cmp[rs] + gl[:, c0 + 1:c0 + 2] * o_slc
                    + gl[:, c0 + 2:c0 + 3] * o_win[rs])
    for j in range(NSA_GROUP):
        o_ref[:, j * 128:(j + 1) * 128] = jnp.where(lo, outs[2 * j], outs[2 * j + 1]).astype(BF16)


def _nsa_prompt(qn, gl, kc, vc, ksb, vsb, kwb, vwb, batch, seq):
    nq = seq // NSA_TQ
    n_c = seq // CMP_BLOCK
    qsp = pl.BlockSpec((NSA_TQ, 512), lambda b, i: (b * nq + i, 0))
    glsp = pl.BlockSpec((NSA_TQ, 128), lambda b, i: (b * nq + i, 0))
    csp = pl.BlockSpec((1, n_c, 128), lambda b, i: (b, 0, 0))
    seqsp = pl.BlockSpec((1, seq, 128), lambda b, i: (b, 0, 0))
    r3 = lambda a: a.reshape(batch, seq, 128)
    return pl.pallas_call(
        _nsa_prompt_body,
        grid=(batch, nq),
        in_specs=[qsp, glsp, csp, csp, seqsp, seqsp, seqsp, seqsp],
        out_specs=qsp,
        out_shape=jax.ShapeDtypeStruct((batch * seq, 512), BF16),
        scratch_shapes=[pltpu.VMEM((8, NSA_TQ, LANES), BF16),
                        pltpu.VMEM((8 * NSA_TQ, LANES), F32),
                        pltpu.VMEM((8 * NSA_TQ, LANES), F32),
                        pltpu.VMEM((8 * NSA_TQ, LANES), F32),
                        pltpu.SMEM((LANES * SLC_BLOCK // NSA_TK,), I32)],
        compiler_params=_params(("parallel", "arbitrary")),
        name="nsa_prompt",
    )(qn, gl, kc, vc, r3(ksb), r3(vsb), r3(kwb), r3(vwb))


CS_CHUNK = 256


def _cumsum_body(gl_ref, ccol_ref, crow_ref, carry_ref):
    @pl.when(pl.program_id(1) == 0)
    def _():
        carry_ref[...] = jnp.zeros(carry_ref.shape, F32)

    r = lax.broadcasted_iota(I32, (CS_CHUNK, CS_CHUNK), 0)
    c = lax.broadcasted_iota(I32, (CS_CHUNK, CS_CHUNK), 1)
    tri = jnp.where(c <= r, 1.0, 0.0).astype(BF16)
    h, m, l = _split3(gl_ref[...])
    cs = (jnp.dot(tri, h, preferred_element_type=F32) + jnp.dot(tri, m, preferred_element_type=F32)
          + jnp.dot(tri, l, preferred_element_type=F32)) + carry_ref[0:1, :]
    ccol_ref[...] = cs
    crow_ref[0] = cs.T[GL_LOGF:GL_LOGF + FOX_HEADS, :]
    carry_ref[...] = jnp.broadcast_to(cs[CS_CHUNK - 1:CS_CHUNK, :], carry_ref.shape)


def _logf_cumsum(gl, batch, seq):
    nch = seq // CS_CHUNK
    return pl.pallas_call(
        _cumsum_body,
        grid=(batch, nch),
        in_specs=[pl.BlockSpec((CS_CHUNK, 128), lambda b, i: (b * nch + i, 0))],
        out_specs=[pl.BlockSpec((CS_CHUNK, 128), lambda b, i: (b * nch + i, 0)),
                   pl.BlockSpec((1, FOX_HEADS, CS_CHUNK), lambda b, i: (b, 0, i))],
        out_shape=[jax.ShapeDtypeStruct((batch * seq, 128), F32),
                   jax.ShapeDtypeStruct((batch, FOX_HEADS, seq), F32)],
        scratch_shapes=[pltpu.VMEM((SUBLANES, 128), F32)],
        compiler_params=_params(("parallel", "arbitrary")),
        name="logf_cumsum",
    )(gl)


FOX_T = 256
FOX_TW = 512


def _fox_prompt_body(q_ref, k_ref, v_ref, ccol_ref, crow_ref, o_ref, m_ref, l_ref, acc_ref):
    qi = pl.program_id(1)
    lane = lax.broadcasted_iota(I32, (FOX_T, LANES), 1)
    lo = lane < HEAD_DIM
    zero_b = jnp.zeros((FOX_T, LANES), BF16)
    d0 = lax.broadcasted_iota(I32, (FOX_T, LANES), 0) - lane
    ccol = ccol_ref[...]
    for hp in range(FOX_HEADS // 2):
        q = q_ref[:, hp * 128:(hp + 1) * 128]
        qs = jnp.concatenate([jnp.where(lo, q, zero_b), jnp.where(lo, zero_b, q)], axis=0)
        cqs = [jnp.broadcast_to(ccol[:, GL_LOGF + 2 * hp + e:GL_LOGF + 2 * hp + e + 1], (FOX_T, LANES))
               for e in range(2)]
        m_ref[...] = jnp.full(m_ref.shape, NEG_INF, F32)
        l_ref[...] = jnp.zeros(l_ref.shape, F32)
        acc_ref[...] = jnp.zeros(acc_ref.shape, F32)

        def tile(k0, width, masked):
            kt = k_ref[pl.ds(k0, width), hp * 128:(hp + 1) * 128]
            vt = v_ref[pl.ds(k0, width), hp * 128:(hp + 1) * 128]
            s_all = lax.dot_general(qs, kt, NT_DIMS, preferred_element_type=F32)
            n_chunks = width // LANES
            for e in range(2):
                rs = slice(e * FOX_T, (e + 1) * FOX_T)
                ck = crow_ref[0, 2 * hp + e:2 * hp + e + 1, pl.ds(k0, width)]
                cols = [(s_all[rs, c * LANES:(c + 1) * LANES] + cqs[e]) - ck[:, c * LANES:(c + 1) * LANES]
                        for c in range(n_chunks)]
                if masked:
                    vis = [d0 >= c * LANES for c in range(n_chunks)]
                    cols = [jnp.where(vis[c], cols[c], NEG_INF) for c in range(n_chunks)]
                mc = cols[0]
                for c in range(1, n_chunks):
                    mc = jnp.maximum(mc, cols[c])
                m_old = m_ref[rs]
                m_new = jnp.maximum(m_old, jnp.max(mc, axis=-1, keepdims=True))
                alpha = jnp.exp(m_old - m_new)
                ps = [jnp.exp(col - m_new) for col in cols]
                if masked:
                    ps = [jnp.where(vis[c], ps[c], 0.0) for c in range(n_chunks)]
                psum = ps[0]
                for c in range(1, n_chunks):
                    psum = psum + ps[c]
                l_ref[rs] = alpha * l_ref[rs] + psum
                pb = jnp.concatenate([p.astype(BF16) for p in ps], axis=1)
                acc_ref[rs] = alpha * acc_ref[rs] + jnp.dot(pb, vt, preferred_element_type=F32)
                m_ref[rs] = m_new

        def wide_tile(t, carry):
            tile(pl.multiple_of(t * FOX_TW, FOX_TW), FOX_TW, False)
            return carry

        lax.fori_loop(0, (qi * FOX_T) // FOX_TW, wide_tile, 0)

        @pl.when(qi % (FOX_TW // FOX_T) == 1)
        def _():
            tile(pl.multiple_of((qi - 1) * FOX_T, FOX_T), FOX_T, False)

        tile(pl.multiple_of(qi * FOX_T, FOX_T), FOX_T, True)
        o = acc_ref[...] / jnp.sum(l_ref[...], axis=-1, keepdims=True)
        o_ref[:, hp * 128:(hp + 1) * 128] = jnp.where(lo, o[:FOX_T], o[FOX_T:]).astype(BF16)


def _fox_prompt(qf, kfb, vfb, ccol, crow, batch, seq):
    nq = seq // FOX_T
    qsp = pl.BlockSpec((FOX_T, 512), lambda b, i: (b * nq + i, 0))
    kvsp = pl.BlockSpec((seq, 512), lambda b, i: (b, 0))
    return pl.pallas_call(
        _fox_prompt_body,
        grid=(batch, nq),
        in_specs=[qsp, kvsp, kvsp,
                  pl.BlockSpec((FOX_T, 128), lambda b, i: (b * nq + i, 0)),
                  pl.BlockSpec((1, FOX_HEADS, seq), lambda b, i: (b, 0, 0))],
        out_specs=qsp,
        out_shape=jax.ShapeDtypeStruct((batch * seq, 512), BF16),
        scratch_shapes=[pltpu.VMEM((2 * FOX_T, LANES), F32),
                        pltpu.VMEM((2 * FOX_T, LANES), F32),
                        pltpu.VMEM((2 * FOX_T, LANES), F32)],
        compiler_params=_params(("parallel", "arbitrary")),
        name="fox_prompt",
    )(qf, kfb, vfb, ccol, crow)


def _outproj_body(on_ref, of_ref, x_ref, g0_ref, b0_ref, wn_ref, wf_ref, g1_ref, b1_ref, y_ref):
    xn = _layer_norm(x_ref[...], g0_ref[...], b0_ref[...])
    a = (jnp.dot(on_ref[...], wn_ref[...], preferred_element_type=F32)
         + jnp.dot(of_ref[...], wf_ref[...], preferred_element_type=F32))
    y_ref[...] = _layer_norm(DN_ALPHA * xn + a, g1_ref[...], b1_ref[...])


def _outproj_ln1(o_n, o_f, x2, g0, b0, w_n, w_f, g1, b1, tm):
    n = x2.shape[0]
    full = lambda a: pl.BlockSpec(a.shape, lambda i: (0,) * a.ndim)
    row = lambda w: pl.BlockSpec((tm, w), lambda i: (i, 0))
    return pl.pallas_call(
        _outproj_body,
        grid=(n // tm,),
        in_specs=[row(512), row(512), row(D_MODEL), full(g0), full(b0), full(w_n), full(w_f), full(g1), full(b1)],
        out_specs=row(D_MODEL),
        out_shape=jax.ShapeDtypeStruct((n, D_MODEL), F32),
        compiler_params=_params(("parallel",)),
        name="outproj_ln1",
    )(o_n, o_f, x2, g0, b0, w_n, w_f, g1, b1)


def _first_max(v, idx_f, big, axes):
    m = v
    for ax in axes:
        m = jnp.max(m, axis=ax, keepdims=True)
    i = jnp.where(v == m, idx_f, big)
    for ax in axes:
        i = jnp.min(i, axis=ax, keepdims=True)
    return m, i


def _router_body(x_ref, wr_ref, br_ref, e4_ref, p4_ref, w4_ref, cnt_ref, carry_ref):
    tm = x_ref.shape[0]

    @pl.when(pl.program_id(0) == 0)
    def _():
        carry_ref[...] = jnp.zeros(carry_ref.shape, F32)

    xb = x_ref[...].astype(BF16)
    logits = lax.dot_general(wr_ref[...], xb, NT_DIMS, preferred_element_type=F32)
    s = jax.nn.sigmoid(logits)
    sb = s + br_ref[...]
    ng, ne = N_EXPERT_GROUPS, EXPERTS_PER_GROUP
    s3 = s.reshape(ng, ne, tm)
    sb3 = sb.reshape(ng, ne, tm)
    e_idx = lax.broadcasted_iota(I32, (ng, ne, tm), 1).astype(F32)
    g_idx3 = lax.broadcasted_iota(I32, (ng, ne, tm), 0).astype(F32)
    flat = g_idx3 * ne + e_idx
    m1, i1 = _first_max(sb3, e_idx, float(ne), (1,))
    m2, _ = _first_max(jnp.where(e_idx == i1, -jnp.inf, sb3), e_idx, float(ne), (1,))
    gs = (m1 + m2)
    g_idx = lax.broadcasted_iota(I32, (ng, 1, tm), 0).astype(F32)
    keep = jnp.zeros((ng, 1, tm), F32)
    for _ in range(TOPK_GROUPS):
        _, gi = _first_max(gs, g_idx, float(ng), (0,))
        hit = g_idx == gi
        keep = jnp.where(hit, 1.0, keep)
        gs = jnp.where(hit, -jnp.inf, gs)
    cand = jnp.where(keep > 0.5, sb3, NEG_INF)
    hits, es, ws = [], [], []
    for _ in range(TOP_K):
        _, fi = _first_max(cand, flat, float(N_EXPERTS), (1, 0))
        hit = flat == fi
        cand = jnp.where(hit, -jnp.inf, cand)
        hits.append(jnp.where(hit, 1.0, 0.0))
        es.append(fi.reshape(1, tm))
        ws.append(jnp.sum(jnp.sum(jnp.where(hit, s3, 0.0), axis=1, keepdims=True), axis=0, keepdims=True).reshape(1, tm))
    wsum = ((ws[0] + ws[1]) + ws[2]) + ws[3]
    sel = (hits[0] + hits[1] + hits[2] + hits[3]).reshape(N_EXPERTS, tm)
    r = lax.broadcasted_iota(I32, (tm, tm), 0)
    c = lax.broadcasted_iota(I32, (tm, tm), 1)
    strict = jnp.where(r < c, 1.0, 0.0).astype(BF16)
    pos = jnp.dot(sel.astype(BF16), strict, preferred_element_type=F32) + carry_ref[:, 0:1]
    pos3 = pos.reshape(ng, ne, tm)
    for k in range(TOP_K):
        e4_ref[k:k + 1, :] = es[k].astype(I32)
        pk = jnp.sum(jnp.sum(hits[k] * pos3, axis=1, keepdims=True), axis=0, keepdims=True).reshape(1, tm)
        p4_ref[k:k + 1, :] = pk.astype(I32)
        w4_ref[k:k + 1, :] = ROUTED_SCALE * ws[k] / wsum
    total = carry_ref[...] + jnp.sum(sel, axis=1, keepdims=True)
    carry_ref[...] = total
    cnt_ref[...] = total.astype(I32)


def _router(x1, wr_t, br_col, tm):
    n = x1.shape[0]
    osp = pl.BlockSpec((TOP_K, tm), lambda i: (0, i))
    return pl.pallas_call(
        _router_body,
        grid=(n // tm,),
        in_specs=[pl.BlockSpec((tm, D_MODEL), lambda i: (i, 0)),
                  pl.BlockSpec(wr_t.shape, lambda i: (0, 0)),
                  pl.BlockSpec(br_col.shape, lambda i: (0, 0))],
        out_specs=[osp, osp, osp, pl.BlockSpec((N_EXPERTS, LANES), lambda i: (0, 0))],
        out_shape=[jax.ShapeDtypeStruct((TOP_K, n), I32), jax.ShapeDtypeStruct((TOP_K, n), I32),
                   jax.ShapeDtypeStruct((TOP_K, n), F32), jax.ShapeDtypeStruct((N_EXPERTS, LANES), I32)],
        scratch_shapes=[pltpu.VMEM((N_EXPERTS, LANES), F32)],
        compiler_params=_params(("arbitrary",)),
        name="moe_router",
    )(x1, wr_t, br_col)


def _row_copy(src, s, dst, d, sem):
    return pltpu.make_async_copy(src.at[pl.ds(s, 1)], dst.at[pl.ds(d, 1)], sem)


def _dispatch_body(dest_ref, x_ref, xb_in_ref, xb_ref, sem):
    del xb_in_ref
    tm = dest_ref.shape[1]

    def start(t, carry):
        for k in range(TOP_K):
            _row_copy(x_ref, t, xb_ref, dest_ref[k, t], sem).start()
        return carry

    lax.fori_loop(0, tm, start, 0)

    def wait(t, carry):
        for k in range(TOP_K):
            _row_copy(x_ref, 0, xb_ref, 0, sem).wait()
        return carry

    lax.fori_loop(0, tm, wait, 0)


def _dispatch(dest4, x1, n_rows, tm):
    n = x1.shape[0]
    xb0 = jnp.zeros((n_rows, D_MODEL), F32)
    return pl.pallas_call(
        _dispatch_body,
        grid=(n // tm,),
        in_specs=[pl.BlockSpec((TOP_K, tm), lambda i: (0, i), memory_space=pltpu.SMEM),
                  pl.BlockSpec((tm, D_MODEL), lambda i: (i, 0)),
                  pl.BlockSpec(memory_space=pl.ANY)],
        out_specs=pl.BlockSpec(memory_space=pl.ANY),
        out_shape=jax.ShapeDtypeStruct((n_rows, D_MODEL), F32),
        scratch_shapes=[pltpu.SemaphoreType.DMA(())],
        input_output_aliases={2: 0},
        compiler_params=pltpu.CompilerParams(dimension_semantics=("arbitrary",), has_side_effects=True,
                                             vmem_limit_bytes=VMEM_LIMIT),
        name="moe_dispatch",
    )(dest4, x1, xb0)


def _expert_body(blk_e_ref, n_used_ref, x_ref, wg_ref, wu_ref, wd_ref, y_ref):
    del blk_e_ref
    i = pl.program_id(0)

    @pl.when(i < n_used_ref[0])
    def _():
        xs = x_ref[...].astype(BF16)
        hg = jnp.dot(xs, wg_ref[0].astype(BF16), preferred_element_type=F32)
        hu = jnp.dot(xs, wu_ref[0].astype(BF16), preferred_element_type=F32)
        h = (jax.nn.silu(hg) * hu).astype(BF16)
        y_ref[...] = jnp.dot(h, wd_ref[0].astype(BF16), preferred_element_type=F32)

    @pl.when(i >= n_used_ref[0])
    def _():
        y_ref[...] = jnp.zeros(y_ref.shape, F32)


def _experts(blk_e, n_used, xb, w_g, w_u, w_d, blk):
    n_rows = xb.shape[0]
    grid_spec = pltpu.PrefetchScalarGridSpec(
        num_scalar_prefetch=2,
        grid=(n_rows // blk,),
        in_specs=[pl.BlockSpec((blk, D_MODEL), lambda i, be, nu: (i, 0)),
                  pl.BlockSpec((1, D_MODEL, D_EXPERT), lambda i, be, nu: (be[i], 0, 0)),
                  pl.BlockSpec((1, D_MODEL, D_EXPERT), lambda i, be, nu: (be[i], 0, 0)),
                  pl.BlockSpec((1, D_EXPERT, D_MODEL), lambda i, be, nu: (be[i], 0, 0))],
        out_specs=pl.BlockSpec((blk, D_MODEL), lambda i, be, nu: (i, 0)),
    )
    return pl.pallas_call(
        _expert_body,
        grid_spec=grid_spec,
        out_shape=jax.ShapeDtypeStruct((n_rows, D_MODEL), F32),
        compiler_params=_params(("arbitrary",)),
        name="moe_experts",
    )(blk_e, n_used, xb, w_g, w_u, w_d)


def _combine_body(dest_ref, w_ref, x_ref, yb_ref, wsg_ref, wsu_ref, wsd_ref, g_ref, b_ref, y_ref, gbuf, sem):
    tm = x_ref.shape[0]

    def start(t, carry):
        for k in range(TOP_K):
            pltpu.make_async_copy(yb_ref.at[pl.ds(dest_ref[k, t], 1)], gbuf.at[k, pl.ds(t, 1)], sem).start()
        return carry

    lax.fori_loop(0, tm, start, 0)
    x = x_ref[...]
    xb = x.astype(BF16)
    hs = jax.nn.silu(jnp.dot(xb, wsg_ref[...], preferred_element_type=F32)) * jnp.dot(xb, wsu_ref[...], preferred_element_type=F32)
    shared = jnp.dot(hs.astype(BF16), wsd_ref[...], preferred_element_type=F32)

    def wait(t, carry):
        for k in range(TOP_K):
            pltpu.make_async_copy(yb_ref.at[pl.ds(0, 1)], gbuf.at[k, pl.ds(0, 1)], sem).wait()
        return carry

    lax.fori_loop(0, tm, wait, 0)
    w = w_ref[...]
    routed = gbuf[0] * w[:, 0:1]
    for k in range(1, TOP_K):
        routed = routed + gbuf[k] * w[:, k:k + 1]
    y_ref[...] = _layer_norm(DN_ALPHA * x + (routed + shared), g_ref[...], b_ref[...])


def _combine(dest4, w4t, x1, yb, wsg, wsu, wsd, g2, b2, tm):
    n = x1.shape[0]
    full = lambda a: pl.BlockSpec(a.shape, lambda i: (0,) * a.ndim)
    return pl.pallas_call(
        _combine_body,
        grid=(n // tm,),
        in_specs=[pl.BlockSpec((TOP_K, tm), lambda i: (0, i), memory_space=pltpu.SMEM),
                  pl.BlockSpec((tm, TOP_K), lambda i: (i, 0)),
                  pl.BlockSpec((tm, D_MODEL), lambda i: (i, 0)),
                  pl.BlockSpec(memory_space=pl.ANY),
                  full(wsg), full(wsu), full(wsd), full(g2), full(b2)],
        out_specs=pl.BlockSpec((tm, D_MODEL), lambda i: (i, 0)),
        out_shape=jax.ShapeDtypeStruct((n, D_MODEL), F32),
        scratch_shapes=[pltpu.VMEM((TOP_K, tm, D_MODEL), F32), pltpu.SemaphoreType.DMA(())],
        compiler_params=_params(("arbitrary",)),
        name="moe_combine",
    )(dest4, w4t, x1, yb, wsg, wsu, wsd, g2, b2)


def _moe_ln2(x1, wr_t, br_col, w_g, w_u, w_d, wsg, wsu, wsd, g2, b2, tm, blk):
    n = x1.shape[0]
    e4, p4, w4, cnt = _router(x1, wr_t, br_col, min(tm, 128))
    counts = cnt[:, 0]
    padded = (counts + blk - 1) // blk * blk
    pend = jnp.cumsum(padded)
    pstart = pend - padded
    dest4 = pstart[e4] + p4
    n_blocks = -(-(n * TOP_K + N_EXPERTS * (blk - 1)) // blk)
    starts = jnp.arange(n_blocks, dtype=I32) * blk
    blk_e = jnp.minimum(jnp.sum((pend[None, :] <= starts[:, None]).astype(I32), axis=1), N_EXPERTS - 1)
    n_used = (pend[-1:] // blk).astype(I32)
    xb = _dispatch(dest4, x1, n_blocks * blk, tm)
    yb = _experts(blk_e, n_used, xb, w_g, w_u, w_d, blk)
    return _combine(dest4, w4.T, x1, yb, wsg, wsu, wsd, g2, b2, tm)


PC_SLABS = 256


def _paged_compress_body(x_ref, w_ref, o_ref):
    acc = jnp.zeros((PC_SLABS, 4 * HEAD_DIM), F32)
    for dp in range(HEAD_DIM // 2):
        a = x_ref[pl.ds(2 * dp, PC_SLABS, stride=HEAD_DIM), :]
        b = x_ref[pl.ds(2 * dp + 1, PC_SLABS, stride=HEAD_DIM), :]
        lhs = jnp.concatenate([a, b], axis=1).astype(BF16)
        acc = acc + jnp.dot(lhs, w_ref[dp], preferred_element_type=F32)
    o_ref[...] = acc


def _paged_compress(cache_t, w_big):
    n_slabs = cache_t.shape[0] * cache_t.shape[1]
    x = cache_t.reshape(n_slabs * HEAD_DIM, PAGE_SIZE)
    return pl.pallas_call(
        _paged_compress_body,
        grid=(n_slabs // PC_SLABS,),
        in_specs=[pl.BlockSpec((PC_SLABS * HEAD_DIM, PAGE_SIZE), lambda i: (i, 0)),
                  pl.BlockSpec(w_big.shape, lambda i: (0, 0, 0))],
        out_specs=pl.BlockSpec((PC_SLABS, 4 * HEAD_DIM), lambda i: (i, 0)),
        out_shape=jax.ShapeDtypeStruct((n_slabs, 4 * HEAD_DIM), F32),
        compiler_params=_params(("parallel",)),
        name="paged_compress",
    )(x, w_big)


def _paged_cmp_weight(w):
    eye = jnp.eye(PAGE_SIZE // CMP_BLOCK, dtype=w.dtype)
    wb = jnp.einsum('ide,nm->dnime', w, eye)
    return wb.reshape(HEAD_DIM // 2, 2 * PAGE_SIZE, 4 * HEAD_DIM).astype(BF16)


QROWS = 16


def _head_slopes(shape, axis):
    h = lax.broadcasted_iota(I32, shape, axis)
    s = jnp.zeros(shape, F32)
    for i in range(NSA_HEADS):
        s = jnp.where(h == i, float(2.0 ** (-(i + 1))), s)
    return s


SEL_NB = SUBLANES // NSA_KV_HEADS


def _nsa_dec_select_body(q_ref, kc_ref, vc_ref, sel_ref, oc_ref, *, past):
    n_c = kc_ref.shape[1]
    half = n_c // 2
    col = lax.broadcasted_iota(I32, (QROWS, n_c), 1)
    cblk = 2 * (col % half) + col // half
    dist = (past - (cblk * CMP_BLOCK + (CMP_BLOCK - 1))).astype(F32)
    mask = dist >= 0
    bias = _head_slopes((QROWS, n_c), 0) * dist
    row = lax.broadcasted_iota(I32, (QROWS, n_c), 0)
    lane = lax.broadcasted_iota(I32, (SUBLANES, LANES), 1)
    srow = lax.broadcasted_iota(I32, (SUBLANES, LANES), 0)
    lane_f = lane.astype(F32)
    lane_o = lax.broadcasted_iota(I32, (NSA_GROUP, LANES), 1)
    forced = (lane[0:1] == 0) | (lane[0:1] == past // SLC_BLOCK)
    score = jnp.full((SUBLANES, LANES), -jnp.inf, F32)
    for i in range(SEL_NB):
        lg = lax.dot_general(q_ref[i], kc_ref[i], NT_DIMS, preferred_element_type=F32)
        lc = jnp.where(mask, lg - bias, NEG_INF)
        m = jnp.max(lc, axis=-1, keepdims=True)
        p = jnp.where(mask, jnp.exp(lc - m), 0.0)
        pc = p / jnp.maximum(jnp.sum(p, axis=-1, keepdims=True), 1e-30)
        o = jnp.dot(pc.astype(BF16), vc_ref[i], preferred_element_type=F32)
        oc_ref[i] = jnp.where(lane_o < HEAD_DIM, o[:NSA_GROUP], o[NSA_GROUP:NSA_HEADS])
        for g in range(NSA_KV_HEADS):
            in_g = (row >= g * NSA_GROUP) & (row < (g + 1) * NSA_GROUP)
            sg = jnp.sum(jnp.where(in_g, pc, 0.0), axis=0, keepdims=True)
            imp = sg[:, :half] + sg[:, half:]
            sc = jnp.where(lane[0:1] * SLC_BLOCK <= past, imp + jnp.where(forced, FORCE_BONUS, 0.0), -1.0)
            score = jnp.where(srow == NSA_KV_HEADS * i + g, jnp.broadcast_to(sc, (SUBLANES, LANES)), score)
    n_before = jnp.sum(jnp.where(score >= FORCE_BONUS, 1.0, 0.0), axis=-1, keepdims=True)
    new_sel = n_before < float(SLC_TOPK)
    order = jnp.zeros((SUBLANES, LANES), F32)
    s = score
    for k in range(SLC_TOPK):
        mk = jnp.max(s, axis=-1, keepdims=True)
        idx = jnp.min(jnp.where(s == mk, lane_f, float(LANES)), axis=-1, keepdims=True)
        order = jnp.where(lane == k, idx, order)
        s = jnp.where(lane_f == idx, -jnp.inf, s)
    order = jnp.where((lane == SLC_TOPK - 1) & new_sel, float(LANES), order)
    sel_ref[0] = order.astype(I32)


def _nsa_dec_select(qmat, kc, vc, past):
    bd, n_c = kc.shape[0], kc.shape[1]
    return pl.pallas_call(
        functools.partial(_nsa_dec_select_body, past=past),
        grid=(bd // SEL_NB,),
        in_specs=[pl.BlockSpec((SEL_NB, QROWS, LANES), lambda b: (b, 0, 0)),
                  pl.BlockSpec((SEL_NB, n_c, LANES), lambda b: (b, 0, 0)),
                  pl.BlockSpec((SEL_NB, n_c, LANES), lambda b: (b, 0, 0))],
        out_specs=[pl.BlockSpec((1, SUBLANES, LANES), lambda b: (b, 0, 0)),
                   pl.BlockSpec((SEL_NB, NSA_GROUP, LANES), lambda b: (b, 0, 0))],
        out_shape=[jax.ShapeDtypeStruct((bd // SEL_NB, SUBLANES, LANES), I32),
                   jax.ShapeDtypeStruct((bd, NSA_GROUP, LANES), F32)],
        compiler_params=_params(("parallel",)),
        name="nsa_decode_select",
    )(qmat, kc, vc)


ND_SLABS = 8
ND_STEPS = SLC_TOPK // ND_SLABS
ND_PAGE, ND_HALF, ND_SLOT, ND_NEW = 0, 32, 64, 96


def _nsa_dec_attend_body(meta_ref, *refs, past):
    ks_refs, vs_refs = refs[:ND_SLABS], refs[ND_SLABS:2 * ND_SLABS]
    (kw_ref, vw_ref, q_ref, nks_ref, nvs_ref, nkw_ref, nvw_ref, o_ref, m_ref, l_ref, acc_ref) = refs[2 * ND_SLABS:]
    bg = pl.program_id(0)
    kh = pl.program_id(1)
    b = bg // NSA_KV_HEADS
    g = bg % NSA_KV_HEADS
    lane = lax.broadcasted_iota(I32, (1, LANES), 1)
    qb = [jnp.broadcast_to(q_ref[0, r], (HEAD_DIM, LANES)) for r in range(NSA_GROUP)]
    slopes = [jnp.where(g == 0, _slope(0, r), _slope(1, r)) for r in range(NSA_GROUP)]

    @pl.when(kh == 0)
    def _():
        m_ref[...] = jnp.full(m_ref.shape, NEG_INF, F32)
        l_ref[...] = jnp.zeros(l_ref.shape, F32)
        acc_ref[...] = jnp.zeros(acc_ref.shape, F32)

    rows = [[] for _ in range(NSA_GROUP)]
    for i in range(ND_SLABS):
        j = g * SLC_TOPK + kh * ND_SLABS + i
        ok = (lane // SLC_BLOCK) == meta_ref[b, ND_HALF + j]
        dist = (past - (meta_ref[b, ND_SLOT + j] * PAGE_SIZE + lane)).astype(F32)
        kt = ks_refs[i][0, 0]
        for r in range(NSA_GROUP):
            rowv = jnp.sum(qb[r] * kt, axis=0, keepdims=True)
            rows[r].append(jnp.where(ok, rowv - slopes[r] * dist, NEG_INF))
    for r in range(NSA_GROUP):
        s = jnp.concatenate(rows[r], axis=0)
        m_old = m_ref[r:r + 1, :]
        m_new = jnp.maximum(m_old, jnp.max(jnp.max(s, axis=-1, keepdims=True), axis=0, keepdims=True))
        alpha = jnp.exp(m_old - m_new)
        p = jnp.where(s > 0.5 * NEG_INF, jnp.exp(s - m_new), 0.0)
        l_ref[r:r + 1, :] = alpha * l_ref[r:r + 1, :] + jnp.sum(jnp.sum(p, axis=-1, keepdims=True), axis=0, keepdims=True)
        acc = alpha * acc_ref[r]
        for i in range(ND_SLABS):
            acc = acc + p[i:i + 1, :] * vs_refs[i][0, 0]
        acc_ref[r] = acc
        m_ref[r:r + 1, :] = m_new

    @pl.when(kh == ND_STEPS - 1)
    def _():
        lane_o = lax.broadcasted_iota(I32, (HEAD_DIM, LANES), 1)
        out = jnp.zeros((HEAD_DIM, LANES), F32)
        new_on = meta_ref[b, ND_NEW + g] > 0
        n_wt = WINDOW // LANES
        wrows = [[] for _ in range(NSA_GROUP)]
        for c in range(n_wt):
            kt = kw_ref[0, :, c * LANES:(c + 1) * LANES]
            dist_i = WINDOW - (c * LANES + lane)
            for r in range(NSA_GROUP):
                rowv = jnp.sum(qb[r] * kt, axis=0, keepdims=True)
                wrows[r].append(jnp.where(dist_i < WINDOW, rowv - slopes[r] * dist_i.astype(F32), NEG_INF))
        for r in range(NSA_GROUP):
            lgn = jnp.where(new_on, jnp.sum(q_ref[0, r] * nks_ref[0], axis=0, keepdims=True), NEG_INF)
            m_old = m_ref[r:r + 1, :]
            m_fin = jnp.maximum(m_old, lgn)
            alpha = jnp.exp(m_old - m_fin)
            pn = jnp.where(new_on, jnp.exp(lgn - m_fin), 0.0)
            den = alpha * l_ref[r:r + 1, :] + pn
            o_col = (alpha * jnp.sum(acc_ref[r], axis=-1, keepdims=True) + pn * nvs_ref[0]) / jnp.maximum(den, 1e-30)
            out = jnp.where(lane_o == 2 * r, jnp.broadcast_to(o_col, (HEAD_DIM, LANES)), out)
            lgw = jnp.sum(q_ref[0, r] * nkw_ref[0], axis=0, keepdims=True)
            lw = jnp.concatenate(wrows[r], axis=0)
            m = jnp.maximum(jnp.max(jnp.max(lw, axis=-1, keepdims=True), axis=0, keepdims=True), lgw)
            p = jnp.where(lw > 0.5 * NEG_INF, jnp.exp(lw - m), 0.0)
            pw = jnp.exp(lgw - m)
            den = jnp.sum(jnp.sum(p, axis=-1, keepdims=True), axis=0, keepdims=True) + pw
            acc = jnp.zeros((HEAD_DIM, LANES), F32)
            for c in range(n_wt):
                acc = acc + p[c:c + 1, :] * vw_ref[0, :, c * LANES:(c + 1) * LANES]
            o_col = (jnp.sum(acc, axis=-1, keepdims=True) + pw * nvw_ref[0]) / jnp.maximum(den, 1e-30)
            out = jnp.where(lane_o == 2 * r + 1, jnp.broadcast_to(o_col, (HEAD_DIM, LANES)), out)
        o_ref[0] = out


def _nsa_dec_meta(page_table, sel, past):
    bd = sel.shape[0]
    n_past_blk = past // SLC_BLOCK
    is_new = sel >= n_past_blk
    sp = jnp.minimum(sel, n_past_blk - 1)
    slot = sp // (PAGE_SIZE // SLC_BLOCK)
    page = jnp.take_along_axis(page_table, slot, axis=1)
    half = jnp.where(is_new, 2, sp % (PAGE_SIZE // SLC_BLOCK))
    has_new = jnp.any(is_new.reshape(bd, NSA_KV_HEADS, SLC_TOPK), axis=-1).astype(I32)
    pad = jnp.zeros((bd, LANES - ND_NEW - NSA_KV_HEADS), I32)
    return jnp.concatenate([page, half, slot, has_new, pad], axis=1).astype(I32)


def _nsa_dec_attend(meta, slc_k_t, slc_v_t, win_k_t, win_v_t, qcol, nks, nvs, nkw, nvw, past):
    n_bg = qcol.shape[0]

    def slab_spec(i):
        def imap(bg, kh, mt):
            g = bg % NSA_KV_HEADS
            return (mt[bg // NSA_KV_HEADS, ND_PAGE + g * SLC_TOPK + kh * ND_SLABS + i], g, 0, 0)
        return pl.BlockSpec((1, 1, HEAD_DIM, PAGE_SIZE), imap)

    slabs = [slab_spec(i) for i in range(ND_SLABS)]
    wsp = pl.BlockSpec((1, HEAD_DIM, WINDOW), lambda bg, kh, mt: (bg, 0, 0))
    csp = pl.BlockSpec((1, HEAD_DIM, 1), lambda bg, kh, mt: (bg, 0, 0))
    grid_spec = pltpu.PrefetchScalarGridSpec(
        num_scalar_prefetch=1,
        grid=(n_bg, ND_STEPS),
        in_specs=slabs + slabs + [wsp, wsp, pl.BlockSpec((1, NSA_GROUP, HEAD_DIM, 1), lambda bg, kh, mt: (bg, 0, 0, 0)),
                                  csp, csp, csp, csp],
        out_specs=pl.BlockSpec((1, HEAD_DIM, LANES), lambda bg, kh, mt: (bg, 0, 0)),
        scratch_shapes=[pltpu.VMEM((NSA_GROUP, 1), F32),
                        pltpu.VMEM((NSA_GROUP, 1), F32),
                        pltpu.VMEM((NSA_GROUP, HEAD_DIM, LANES), F32)],
    )
    return pl.pallas_call(
        functools.partial(_nsa_dec_attend_body, past=past),
        grid_spec=grid_spec,
        out_shape=jax.ShapeDtypeStruct((n_bg, HEAD_DIM, LANES), F32),
        compiler_params=_params(("parallel", "arbitrary")),
        name="nsa_decode_attend",
    )(meta, *([slc_k_t] * ND_SLABS), *([slc_v_t] * ND_SLABS), win_k_t, win_v_t, qcol, nks, nvs, nkw, nvw)


FD_PPS = 8


def _fox_dec_body(pt_ref, *refs):
    del pt_ref
    k_refs, v_refs, lf_refs = refs[:FD_PPS], refs[FD_PPS:2 * FD_PPS], refs[2 * FD_PPS:3 * FD_PPS]
    q_ref, nk_ref, nv_ref, nlf_ref, o_ref, qb_ref, m_ref, l_ref, acc_ref, carry_ref = refs[3 * FD_PPS:]
    j = pl.program_id(1)
    nh = FOX_HEADS
    lane3 = lax.broadcasted_iota(I32, (nh, HEAD_DIM, LANES), 2)

    @pl.when(j == 0)
    def _():
        q = q_ref[0]
        qb_ref[...] = jnp.broadcast_to(q, (nh, HEAD_DIM, LANES))
        m_ref[...] = jnp.sum(q * nk_ref[0], axis=1)
        l_ref[...] = jnp.ones(l_ref.shape, F32)
        acc_ref[...] = jnp.where(lane3 == 0, jnp.broadcast_to(nv_ref[0], (nh, HEAD_DIM, LANES)), 0.0)
        carry_ref[...] = nlf_ref[0]

    r = lax.broadcasted_iota(I32, (PAGE_SIZE, PAGE_SIZE), 0)
    c = lax.broadcasted_iota(I32, (PAGE_SIZE, PAGE_SIZE), 1)
    later = jnp.where(r > c, 1.0, 0.0).astype(BF16)
    for i in range(FD_PPS):
        lf = lf_refs[i][0, 0]
        h3, m3, l3 = _split3(lf)
        sfx = (jnp.dot(h3, later, preferred_element_type=F32) + jnp.dot(m3, later, preferred_element_type=F32)
               + jnp.dot(l3, later, preferred_element_type=F32))
        bias = carry_ref[...] + sfx
        carry_ref[...] = carry_ref[...] + jnp.sum(lf, axis=-1, keepdims=True)
        rows = [jnp.sum(qb_ref[h] * k_refs[i][0, 0, h], axis=0, keepdims=True) for h in range(nh)]
        s = jnp.concatenate(rows, axis=0) + bias
        m_old = m_ref[...]
        m_new = jnp.maximum(m_old, jnp.max(s, axis=-1, keepdims=True))
        alpha = jnp.exp(m_old - m_new)
        p = jnp.exp(s - m_new)
        l_ref[...] = alpha * l_ref[...] + jnp.sum(p, axis=-1, keepdims=True)
        m_ref[...] = m_new
        for h in range(nh):
            acc_ref[h] = alpha[h:h + 1, :] * acc_ref[h] + p[h:h + 1, :] * v_refs[i][0, 0, h]

    @pl.when(j == pl.num_programs(1) - 1)
    def _():
        lane_o = lax.broadcasted_iota(I32, (HEAD_DIM, LANES), 1)
        out = jnp.zeros((HEAD_DIM, LANES), F32)
        l = l_ref[...]
        for h in range(nh):
            o_col = jnp.sum(acc_ref[h], axis=-1, keepdims=True) / l[h:h + 1, :]
            out = jnp.where(lane_o == h, jnp.broadcast_to(o_col, (HEAD_DIM, LANES)), out)
        o_ref[0] = out


def _fox_decode(page_table, fox_k_t, fox_v_t, fox_lf_t, qcol, nk, nv, nlf):
    bd, n_pages = page_table.shape
    n_steps = n_pages // FD_PPS

    def page(i):
        return lambda b, j, pt: (0, pt[b, n_pages - 1 - (j * FD_PPS + i)], 0, 0, 0)

    def page4(i):
        return lambda b, j, pt: (0, pt[b, n_pages - 1 - (j * FD_PPS + i)], 0, 0)

    kv_specs = [pl.BlockSpec((1, 1, FOX_HEADS, HEAD_DIM, PAGE_SIZE), page(i)) for i in range(FD_PPS)]
    lf_specs = [pl.BlockSpec((1, 1, FOX_HEADS, PAGE_SIZE), page4(i)) for i in range(FD_PPS)]
    csp = pl.BlockSpec((1, FOX_HEADS, HEAD_DIM, 1), lambda b, j, pt: (b, 0, 0, 0))
    grid_spec = pltpu.PrefetchScalarGridSpec(
        num_scalar_prefetch=1,
        grid=(bd, n_steps),
        in_specs=kv_specs + kv_specs + lf_specs + [csp, csp, csp, pl.BlockSpec((1, FOX_HEADS, 1), lambda b, j, pt: (b, 0, 0))],
        out_specs=pl.BlockSpec((1, HEAD_DIM, LANES), lambda b, j, pt: (b, 0, 0)),
        scratch_shapes=[pltpu.VMEM((FOX_HEADS, HEAD_DIM, LANES), F32),
                        pltpu.VMEM((FOX_HEADS, 1), F32),
                        pltpu.VMEM((FOX_HEADS, 1), F32),
                        pltpu.VMEM((FOX_HEADS, HEAD_DIM, LANES), F32),
                        pltpu.VMEM((FOX_HEADS, 1), F32)],
    )
    return pl.pallas_call(
        _fox_dec_body,
        grid_spec=grid_spec,
        out_shape=jax.ShapeDtypeStruct((bd, HEAD_DIM, LANES), F32),
        compiler_params=_params(("parallel", "arbitrary")),
        name="fox_decode",
    )(page_table, *([fox_k_t] * FD_PPS), *([fox_v_t] * FD_PPS), *([fox_lf_t] * FD_PPS), qcol, nk, nv, nlf)


def _outproj_dec_body(oc_ref, os_ref, ow_ref, gc_ref, gs_ref, gw_ref, of_ref, x_ref, g0_ref, b0_ref,
                      wn_ref, wf_ref, g1_ref, b1_ref, y_ref):
    xn = _layer_norm(x_ref[...], g0_ref[...], b0_ref[...])
    o_n = gc_ref[...] * oc_ref[...] + gs_ref[...] * os_ref[...] + gw_ref[...] * ow_ref[...]
    a = (jnp.dot(o_n.astype(BF16), wn_ref[...], preferred_element_type=F32)
         + jnp.dot(of_ref[...].astype(BF16), wf_ref[...], preferred_element_type=F32))
    y_ref[...] = _layer_norm(DN_ALPHA * xn + a, g1_ref[...], b1_ref[...])


def _outproj_dec(oc, osl, ow, gc, gs, gw, o_f, x2, g0, b0, w_n, w_f, g1, b1):
    n = x2.shape[0]
    args = (oc, osl, ow, gc, gs, gw, o_f, x2, g0, b0, w_n, w_f, g1, b1)
    return pl.pallas_call(
        _outproj_dec_body,
        grid=(1,),
        in_specs=[pl.BlockSpec(a.shape, lambda i: (0,) * a.ndim) for a in args],
        out_specs=pl.BlockSpec((n, D_MODEL), lambda i: (0, 0)),
        out_shape=jax.ShapeDtypeStruct((n, D_MODEL), F32),
        compiler_params=_params(("arbitrary",)),
        name="outproj_ln1_decode",
    )(*args)


def kernel(x_prompt, x_sample, cache_fox_k, cache_fox_v, cache_fox_logf, cache_cmp_k, cache_cmp_v, cache_slc_k, cache_slc_v, state_win_k, state_win_v, page_table, ln_in_g, ln_in_b, w_in, b_in, w_cmp_k, w_cmp_v, w_out, ln1_g, ln1_b, w_router, b_router, w_exp_gate, w_exp_up, w_exp_down, w_sh_gate, w_sh_up, w_sh_down, ln2_g, ln2_b):
    batch, seq, _ = x_prompt.shape
    bd = x_sample.shape[0]
    l = 0
    g0, b0 = ln_in_g.reshape(1, D_MODEL), ln_in_b.reshape(1, D_MODEL)
    wp, bp = _prep_proj_weights(w_in[l], b_in[l])
    wck, wcv = _blockdiag_cmp_weight(w_cmp_k[l]), _blockdiag_cmp_weight(w_cmp_v[l])
    nsa_perm = _nsa_col_perm()
    w_on = jnp.concatenate([w_out[l][nsa_perm[c]:nsa_perm[c] + HEAD_DIM] for c in range(0, NSA_HEADS * HEAD_DIM, HEAD_DIM)],
                           axis=0).astype(BF16)
    w_of = w_out[l][NSA_HEADS * HEAD_DIM:].astype(BF16)
    wr_t = w_router[l].T.astype(BF16)
    br_col = b_router[l].reshape(N_EXPERTS, 1)
    wsg, wsu, wsd = w_sh_gate[l].astype(BF16), w_sh_up[l].astype(BF16), w_sh_down[l].astype(BF16)
    g1, b1, g2, b2 = ln1_g[l:l + 1], ln1_b[l:l + 1], ln2_g[l:l + 1], ln2_b[l:l + 1]

    xp2 = x_prompt.reshape(batch * seq, D_MODEL)
    (qn, qf, kfb, vfb, ksb, vsb, kwb, vwb, fk, fv, ck, cv, sk, sv, wk, wv, gl) = _project(xp2, g0, b0, wp, bp, 512)
    kc, vc = _compress_prompt(ck, cv, wck, wcv, batch, seq)
    o_n = _nsa_prompt(qn, gl, kc, vc, ksb, vsb, kwb, vwb, batch, seq)
    ccol, crow = _logf_cumsum(gl, batch, seq)
    o_f = _fox_prompt(qf, kfb, vfb, ccol, crow, batch, seq)
    x1 = _outproj_ln1(o_n, o_f, xp2, g0, b0, w_on, w_of, g1, b1, 512)
    y_p = _moe_ln2(x1, wr_t, br_col, w_exp_gate[l], w_exp_up[l], w_exp_down[l], wsg, wsu, wsd, g2, b2, 256, 256)

    st5 = lambda a, h: a.reshape(1, batch, seq, h, HEAD_DIM)
    keep_p = min(WINDOW, seq)
    p_outs = (st5(fk, FOX_HEADS), st5(fv, FOX_HEADS),
              gl[:, GL_LOGF:GL_LOGF + FOX_HEADS].reshape(1, batch, seq, FOX_HEADS),
              st5(ck, NSA_KV_HEADS), st5(cv, NSA_KV_HEADS), st5(sk, NSA_KV_HEADS), st5(sv, NSA_KV_HEADS),
              st5(wk, NSA_KV_HEADS)[:, :, seq - keep_p:], st5(wv, NSA_KV_HEADS)[:, :, seq - keep_p:])
    n_pages = page_table.shape[1]
    n_phys = cache_cmp_k.shape[1]
    past = n_pages * PAGE_SIZE
    assert x_sample.shape[1] == 1 and past // SLC_BLOCK == LANES and state_win_k.shape[2] == WINDOW
    xs2 = x_sample.reshape(bd, D_MODEL)
    (qn_s, qf_s, _, _, _, _, _, _, fk_s, fv_s, ck_s, cv_s, sk_s, sv_s, wk_s, wv_s, gl_s) = _project(xs2, g0, b0, wp, bp, bd)

    page_minor = lambda c: jnp.transpose(c[l], (0, 2, 3, 1))
    kc_phys = _paged_compress(page_minor(cache_cmp_k), _paged_cmp_weight(w_cmp_k[l]))
    vc_phys = _paged_compress(page_minor(cache_cmp_v), _paged_cmp_weight(w_cmp_v[l]))

    def seq_blocks(c_phys):
        blocks_per_page = PAGE_SIZE // CMP_BLOCK
        c = c_phys.reshape(n_phys, NSA_KV_HEADS, blocks_per_page, HEAD_DIM)[page_table]
        c = c.transpose(0, 1, 3, 2, 4).reshape(bd, n_pages * blocks_per_page // 2, 2, 128)
        return c.transpose(0, 2, 1, 3).reshape(bd, n_pages * blocks_per_page, 128).astype(BF16)

    q4 = qn_s.reshape(bd, NSA_GROUP, NSA_KV_HEADS, HEAD_DIM).transpose(0, 2, 1, 3)
    zq = jnp.zeros_like(q4[:, 0])
    qmat = jnp.concatenate([jnp.concatenate([q4[:, 0], zq], axis=-1), jnp.concatenate([zq, q4[:, 1]], axis=-1)], axis=1)
    qmat = jnp.pad(qmat, ((0, 0), (0, QROWS - NSA_HEADS), (0, 0)))
    sel, o_cmp = _nsa_dec_select(qmat, seq_blocks(kc_phys), seq_blocks(vc_phys), past)
    sel2 = sel[:, :, :SLC_TOPK].reshape(bd, NSA_KV_HEADS * SLC_TOPK)
    col = lambda a, h: a.reshape(bd, h, HEAD_DIM, 1)
    gcol = lambda a: a.reshape(bd * NSA_KV_HEADS, HEAD_DIM, 1)
    win_minor = lambda c: jnp.transpose(c[l], (0, 2, 3, 1)).reshape(bd * NSA_KV_HEADS, HEAD_DIM, WINDOW)
    ocols = _nsa_dec_attend(_nsa_dec_meta(page_table, sel2, past), page_minor(cache_slc_k), page_minor(cache_slc_v),
                            win_minor(state_win_k), win_minor(state_win_v),
                            q4.astype(F32).reshape(bd * NSA_KV_HEADS, NSA_GROUP, HEAD_DIM, 1),
                            gcol(sk_s), gcol(sv_s), gcol(wk_s), gcol(wv_s), past)
    ocols = ocols.reshape(bd, NSA_KV_HEADS, HEAD_DIM, LANES)
    to_cols = lambda a: a.reshape(bd, NSA_KV_HEADS, NSA_GROUP, HEAD_DIM).transpose(0, 2, 1, 3).reshape(bd, NSA_HEADS * HEAD_DIM)
    oc = o_cmp.reshape(bd, NSA_HEADS * HEAD_DIM)
    osl = to_cols(ocols[..., 0:2 * NSA_GROUP:2].transpose(0, 1, 3, 2))
    ow = to_cols(ocols[..., 1:2 * NSA_GROUP:2].transpose(0, 1, 3, 2))
    gates = gl_s[:, :GL_LOGF].reshape(bd, NSA_KV_HEADS, NSA_GROUP, N_BRANCH)
    gexp = lambda br: to_cols(jnp.broadcast_to(gates[..., br:br + 1], (bd, NSA_KV_HEADS, NSA_GROUP, HEAD_DIM)))

    fox_minor = lambda c: jnp.transpose(c, (0, 1, 3, 4, 2))
    of_cols = _fox_decode(page_table, fox_minor(cache_fox_k), fox_minor(cache_fox_v),
                          jnp.transpose(cache_fox_logf, (0, 1, 3, 2)),
                          col(qf_s.astype(F32), FOX_HEADS), col(fk_s, FOX_HEADS), col(fv_s, FOX_HEADS),
                          gl_s[:, GL_LOGF:GL_LOGF + FOX_HEADS].reshape(bd, FOX_HEADS, 1))
    of_s = of_cols[:, :, :FOX_HEADS].transpose(0, 2, 1).reshape(bd, FOX_HEADS * HEAD_DIM)
    x1_s = _outproj_dec(oc, osl, ow, gexp(0), gexp(1), gexp(2), of_s, xs2, g0, b0, w_on, w_of, g1, b1)
    y_s = _moe_ln2(x1_s, wr_t, br_col, w_exp_gate[l], w_exp_up[l], w_exp_down[l], wsg, wsu, wsd, g2, b2, 128, 128)

    ss5 = lambda a, h: a.reshape(1, bd, 1, h, HEAD_DIM)
    keep_s = min(WINDOW, past + 1)
    win_new = lambda buf, new: jnp.concatenate([buf, ss5(new, NSA_KV_HEADS)], axis=2)[:, :, buf.shape[2] + 1 - keep_s:]
    s_outs = (ss5(fk_s, FOX_HEADS), ss5(fv_s, FOX_HEADS),
              gl_s[:, GL_LOGF:GL_LOGF + FOX_HEADS].reshape(1, bd, 1, FOX_HEADS),
              ss5(ck_s, NSA_KV_HEADS), ss5(cv_s, NSA_KV_HEADS), ss5(sk_s, NSA_KV_HEADS), ss5(sv_s, NSA_KV_HEADS),
              win_new(state_win_k, wk_s), win_new(state_win_v, wv_s))
    return (y_p.reshape(batch, seq, D_MODEL), y_s.reshape(bd, 1, D_MODEL)) + p_outs + s_outs
```

```python
import functools

import numpy as np
import jax
import jax.numpy as jnp
from jax import lax
from jax.experimental import pallas as pl
from jax.experimental.pallas import tpu as pltpu

F32 = jnp.float32
BF16 = jnp.bfloat16
I32 = jnp.int32

D_MODEL = 1024
HEAD_DIM = 64
NSA_HEADS = 8
NSA_KV_HEADS = 2
NSA_GROUP = NSA_HEADS // NSA_KV_HEADS
FOX_HEADS = 8
CMP_BLOCK = 32
SLC_BLOCK = 64
SLC_TOPK = 16
WINDOW = 512
N_BRANCH = 3
FORCE_BONUS = 1000.0
N_EXPERTS = 64
N_EXPERT_GROUPS = 8
EXPERTS_PER_GROUP = N_EXPERTS // N_EXPERT_GROUPS
TOPK_GROUPS = 4
TOP_K = 4
D_EXPERT = 256
D_SHARED = 256
ROUTED_SCALE = 2.5
LN_EPS = 1e-5
DEPTH = 1
DN_ALPHA = (2.0 * DEPTH) ** 0.25
NEG_INF = -1e30
PAGE_SIZE = 128

LANES = 128
SUBLANES = 8
VMEM_LIMIT = 56 * 1024 * 1024

C_QN, C_QF, C_KF, C_VF = 0, 512, 1024, 1536
C_KC, C_VC, C_KS, C_VS, C_KW, C_VW, C_GL = 2048, 2176, 2304, 2432, 2560, 2688, 2816
N_PROJ = 2944
GL_LOGF = NSA_HEADS * N_BRANCH

R_QN, R_KC, R_VC, R_KS, R_VS, R_KW, R_VW, R_GN, R_QF, R_KF, R_VF, R_FF = (
    0, 512, 640, 768, 896, 1024, 1152, 1280, 1304, 1816, 2328, 2840)

NT_DIMS = (((1,), (1,)), ((), ()))


def _slope(g, r):
    return float(2.0 ** (-(g * NSA_GROUP + r + 1)))


def _nsa_col_perm():
    perm = np.zeros(NSA_HEADS * HEAD_DIM, np.int32)
    for j in range(NSA_GROUP):
        for g in range(NSA_KV_HEADS):
            for d in range(HEAD_DIM):
                perm[j * 128 + g * 64 + d] = (g * NSA_GROUP + j) * HEAD_DIM + d
    return perm


def _proj_perm():
    perm = np.full(N_PROJ, -1, np.int64)
    scale = np.ones(N_PROJ, np.float32)
    perm[C_QN:C_QN + 512] = R_QN + _nsa_col_perm()
    scale[C_QN:C_QN + 512] = HEAD_DIM ** -0.5
    perm[C_QF:C_QF + 512] = R_QF + np.arange(512)
    scale[C_QF:C_QF + 512] = HEAD_DIM ** -0.5
    perm[C_KF:C_KF + 512] = R_KF + np.arange(512)
    perm[C_VF:C_VF + 512] = R_VF + np.arange(512)
    for c, r in ((C_KC, R_KC), (C_VC, R_VC), (C_KS, R_KS), (C_VS, R_VS), (C_KW, R_KW), (C_VW, R_VW)):
        perm[c:c + 128] = r + np.arange(128)
    perm[C_GL:C_GL + 24] = R_GN + np.arange(24)
    perm[C_GL + 24:C_GL + 32] = R_FF + np.arange(8)
    return perm, scale


def _prep_proj_weights(w_in, b_in):
    perm, scale = _proj_perm()
    w_parts, b_parts = [], []
    c = 0
    while c < N_PROJ:
        e = c + 1
        while e < N_PROJ and (perm[e] == perm[e - 1] + 1 if perm[c] >= 0 else perm[e] < 0) and scale[e] == scale[c]:
            e += 1
        if perm[c] >= 0:
            w_parts.append(w_in[:, perm[c]:perm[c] + e - c] * float(scale[c]))
            b_parts.append(b_in[perm[c]:perm[c] + e - c] * float(scale[c]))
        else:
            w_parts.append(jnp.zeros((w_in.shape[0], e - c), w_in.dtype))
            b_parts.append(jnp.zeros((e - c,), b_in.dtype))
        c = e
    wp = jnp.concatenate(w_parts, axis=1).astype(BF16)
    bp = jnp.concatenate(b_parts).reshape(1, N_PROJ)
    return wp, bp


def _layer_norm(x, g, b):
    mu = jnp.mean(x, axis=-1, keepdims=True)
    xc = x - mu
    var = jnp.mean(xc * xc, axis=-1, keepdims=True)
    return xc * lax.rsqrt(var + LN_EPS) * g + b


def _split3(x):
    h = x.astype(BF16)
    r = x - h.astype(F32)
    m = r.astype(BF16)
    l = (r - m.astype(F32)).astype(BF16)
    return h, m, l


def _params(sem):
    return pltpu.CompilerParams(dimension_semantics=sem, vmem_limit_bytes=VMEM_LIMIT)


def _proj_body(x_ref, g_ref, b_ref, w_ref, bias_ref,
               qn_ref, qf_ref, kfb_ref, vfb_ref, ksb_ref, vsb_ref, kwb_ref, vwb_ref,
               fk_ref, fv_ref, ck_ref, cv_ref, sk_ref, sv_ref, wk_ref, wv_ref, gl_ref):
    xn = _layer_norm(x_ref[...], g_ref[...], b_ref[...])
    xb = xn.astype(BF16)

    def sec(c, n):
        return jnp.dot(xb, w_ref[:, c:c + n], preferred_element_type=F32) + bias_ref[:, c:c + n]

    qn_ref[...] = sec(C_QN, 512).astype(BF16)
    qf_ref[...] = sec(C_QF, 512).astype(BF16)
    h = sec(C_KF, 512)
    fk_ref[...] = h
    kfb_ref[...] = h.astype(BF16)
    h = sec(C_VF, 512)
    fv_ref[...] = h
    vfb_ref[...] = h.astype(BF16)
    ck_ref[...] = sec(C_KC, 128)
    cv_ref[...] = sec(C_VC, 128)
    for c, f_ref, b_ref2 in ((C_KS, sk_ref, ksb_ref), (C_VS, sv_ref, vsb_ref),
                             (C_KW, wk_ref, kwb_ref), (C_VW, wv_ref, vwb_ref)):
        h = sec(c, 128)
        f_ref[...] = h
        b_ref2[...] = h.astype(BF16)
    h = sec(C_GL, 128)
    lane = lax.broadcasted_iota(I32, h.shape, 1)
    gl_ref[...] = jnp.where(lane < GL_LOGF, jax.nn.sigmoid(h), jax.nn.log_sigmoid(h))


def _project(x2, ln_g, ln_b, wp, bp, tm):
    n = x2.shape[0]
    bf = lambda w: jax.ShapeDtypeStruct((n, w), BF16)
    ff = lambda w: jax.ShapeDtypeStruct((n, w), F32)
    out_shape = [bf(512), bf(512), bf(512), bf(512), bf(128), bf(128), bf(128), bf(128),
                 ff(512), ff(512), ff(128), ff(128), ff(128), ff(128), ff(128), ff(128), ff(128)]
    row = lambda w: pl.BlockSpec((tm, w), lambda i: (i, 0))
    full = lambda a: pl.BlockSpec(a.shape, lambda i: (0,) * a.ndim)
    return pl.pallas_call(
        _proj_body,
        grid=(n // tm,),
        in_specs=[row(D_MODEL), full(ln_g), full(ln_b), full(wp), full(bp)],
        out_specs=[row(s.shape[1]) for s in out_shape],
        out_shape=out_shape,
        compiler_params=_params(("parallel",)),
        name="ln_in_proj",
    )(x2, ln_g, ln_b, wp, bp)


def _compress_body(xk_ref, xv_ref, wk_ref, wv_ref, kc_ref, vc_ref):
    half = kc_ref.shape[1] // 2
    for x_ref, w_ref, o_ref in ((xk_ref, wk_ref, kc_ref), (xv_ref, wv_ref, vc_ref)):
        for par in range(2):
            acc = jnp.zeros((half, 128), F32)
            for i in range(CMP_BLOCK):
                rows = x_ref[pl.ds(par * CMP_BLOCK + i, half, stride=2 * CMP_BLOCK), :]
                acc = acc + jnp.dot(rows.astype(BF16), w_ref[i], preferred_element_type=F32)
            o_ref[0, par * half:(par + 1) * half, :] = acc.astype(BF16)


def _compress_prompt(ck, cv, wck, wcv, batch, seq):
    n_c = seq // CMP_BLOCK
    rows = pl.BlockSpec((seq, 128), lambda b: (b, 0))
    wsp = pl.BlockSpec((CMP_BLOCK, 128, 128), lambda b: (0, 0, 0))
    osp = pl.BlockSpec((1, n_c, 128), lambda b: (b, 0, 0))
    return pl.pallas_call(
        _compress_body,
        grid=(batch,),
        in_specs=[rows, rows, wsp, wsp],
        out_specs=[osp, osp],
        out_shape=[jax.ShapeDtypeStruct((batch, n_c, 128), BF16)] * 2,
        compiler_params=_params(("parallel",)),
        name="compress_prompt",
    )(ck, cv, wck, wcv)


def _blockdiag_cmp_weight(w):
    z = jnp.zeros_like(w)
    top = jnp.concatenate([w, z], axis=2)
    bot = jnp.concatenate([z, w], axis=2)
    return jnp.concatenate([top, bot], axis=1).astype(BF16)


NSA_TQ = 128
NSA_TK = 512


def _top16_mask(score, lane_f):
    sel = jnp.zeros(score.shape, F32)
    s = score
    for _ in range(SLC_TOPK):
        m = jnp.max(s, axis=-1, keepdims=True)
        idx = jnp.min(jnp.where(s == m, lane_f, float(LANES)), axis=-1, keepdims=True)
        hit = lane_f == idx
        sel = jnp.where(hit, 1.0, sel)
        s = jnp.where(hit, -jnp.inf, s)
    return sel


def _nsa_prompt_body(q_ref, gl_ref, kc_ref, vc_ref, ks_ref, vs_ref, kw_ref, vw_ref, o_ref,
                     qs_ref, m_ref, l_ref, acc_ref, flags_ref):
    i = pl.program_id(1)
    q0 = i * NSA_TQ
    n_c = kc_ref.shape[1]
    lane = lax.broadcasted_iota(I32, (NSA_TQ, LANES), 1)
    row = lax.broadcasted_iota(I32, (NSA_TQ, LANES), 0)
    lo = lane < HEAD_DIM
    zero_b = jnp.zeros((NSA_TQ, LANES), BF16)
    for j in range(NSA_GROUP):
        qj = q_ref[:, j * 128:(j + 1) * 128]
        qs_ref[2 * j] = jnp.where(lo, qj, zero_b)
        qs_ref[2 * j + 1] = jnp.where(lo, zero_b, qj)
    qs = qs_ref[...].reshape(8 * NSA_TQ, LANES)
    slopes = [_slope(rb % 2, rb // 2) for rb in range(8)]

    lc_all = lax.dot_general(qs, kc_ref[0], NT_DIMS, preferred_element_type=F32)
    col = lax.broadcasted_iota(I32, (NSA_TQ, n_c), 1)
    half = n_c // 2
    cblk = 2 * (col % half) + col // half
    pos_c = q0 + lax.broadcasted_iota(I32, (NSA_TQ, n_c), 0)
    dist_c = (pos_c - (cblk * CMP_BLOCK + (CMP_BLOCK - 1))).astype(F32)
    mask_c = dist_c >= 0
    psum = [jnp.zeros((NSA_TQ, n_c), F32), jnp.zeros((NSA_TQ, n_c), F32)]
    pcs = []
    for rb in range(8):
        lc = lc_all[rb * NSA_TQ:(rb + 1) * NSA_TQ] - slopes[rb] * dist_c
        lc = jnp.where(mask_c, lc, NEG_INF)
        m = jnp.max(lc, axis=-1, keepdims=True)
        p = jnp.where(mask_c, jnp.exp(lc - m), 0.0)
        pc = p / jnp.maximum(jnp.sum(p, axis=-1, keepdims=True), 1e-30)
        psum[rb % 2] = psum[rb % 2] + pc
        pcs.append(pc.astype(BF16))
    o_cmp = jnp.dot(jnp.concatenate(pcs, axis=0), vc_ref[0], preferred_element_type=F32)

    lane_f = lane.astype(F32)
    pos = q0 + row
    forced = (lane == 0) | (lane == pos // SLC_BLOCK)
    visible = lane * SLC_BLOCK <= pos
    sels = []
    for g in range(NSA_KV_HEADS):
        imp = psum[g][:, :half] + psum[g][:, half:]
        score = jnp.where(visible, imp + jnp.where(forced, FORCE_BONUS, 0.0), -1.0)
        sels.append(_top16_mask(score, lane_f))
    selstack = jnp.concatenate(sels, axis=0).astype(BF16)

    m_ref[...] = jnp.full(m_ref.shape, NEG_INF, F32)
    l_ref[...] = jnp.zeros(l_ref.shape, F32)
    acc_ref[...] = jnp.zeros(acc_ref.shape, F32)
    n_tiles = (q0 + NSA_TQ + NSA_TK - 1) // NSA_TK
    blk_per_tile = NSA_TK // SLC_BLOCK
    n_chunks = NSA_TK // LANES
    blockany = jnp.max(jnp.maximum(sels[0], sels[1]), axis=0, keepdims=True)
    tile_of_blk = lax.broadcasted_iota(I32, (1, LANES), 1) // blk_per_tile
    for t in range(LANES // blk_per_tile):
        flags_ref[t] = (jnp.max(jnp.where(tile_of_blk == t, blockany, 0.0)) > 0.5).astype(I32)
    e_row = lax.broadcasted_iota(I32, (LANES, NSA_TK), 0)
    e_col = lax.broadcasted_iota(I32, (LANES, NSA_TK), 1) // SLC_BLOCK
    d0 = row - lane

    def slc_tile(t, causal):
        k0 = pl.multiple_of(t * NSA_TK, NSA_TK)
        kt = ks_ref[0, pl.ds(k0, NSA_TK), :]
        vt = vs_ref[0, pl.ds(k0, NSA_TK), :]
        s_all = lax.dot_general(qs, kt, NT_DIMS, preferred_element_type=F32)
        expand = jnp.where(e_row == e_col + t * blk_per_tile, 1.0, 0.0).astype(BF16)
        mexp = jnp.dot(selstack, expand, preferred_element_type=F32)
        dist_i = [d0 + (q0 - k0 - c * LANES) for c in range(n_chunks)]
        dist = [d.astype(F32) for d in dist_i]
        oks = []
        for g in range(NSA_KV_HEADS):
            mg = [mexp[g * NSA_TQ:(g + 1) * NSA_TQ, c * LANES:(c + 1) * LANES] for c in range(n_chunks)]
            if causal:
                mg = [jnp.where(dist_i[c] >= 0, mg[c], 0.0) for c in range(n_chunks)]
            oks.append([m > 0.5 for m in mg])
        for rb in range(8):
            rs = slice(rb * NSA_TQ, (rb + 1) * NSA_TQ)
            cols = [jnp.where(oks[rb % 2][c], s_all[rs, c * LANES:(c + 1) * LANES] - slopes[rb] * dist[c], NEG_INF)
                    for c in range(n_chunks)]
            mc = cols[0]
            for c in range(1, n_chunks):
                mc = jnp.maximum(mc, cols[c])
            m_old = m_ref[rs]
            m_new = jnp.maximum(m_old, jnp.max(mc, axis=-1, keepdims=True))
            alpha = jnp.exp(m_old - m_new)
            ps = [jnp.exp(col - m_new) for col in cols]
            psum_l = ps[0]
            for c in range(1, n_chunks):
                psum_l = psum_l + ps[c]
            l_ref[rs] = alpha * l_ref[rs] + psum_l
            pb = jnp.concatenate([p.astype(BF16) for p in ps], axis=1)
            acc_ref[rs] = alpha * acc_ref[rs] + jnp.dot(pb, vt, preferred_element_type=F32)
            m_ref[rs] = m_new

    def maybe_tile(t, carry):
        @pl.when(flags_ref[t] > 0)
        def _():
            slc_tile(t, False)
        return carry

    lax.fori_loop(0, n_tiles - 1, maybe_tile, 0)
    slc_tile(n_tiles - 1, True)

    tw = WINDOW + NSA_TQ
    ws = pl.multiple_of(jnp.maximum(q0 - WINDOW, 0), NSA_TQ)
    kwt = kw_ref[0, pl.ds(ws, tw), :]
    vwt = vw_ref[0, pl.ds(ws, tw), :]
    sw_all = lax.dot_general(qs, kwt, NT_DIMS, preferred_element_type=F32)
    dist_wi = (q0 + lax.broadcasted_iota(I32, (NSA_TQ, tw), 0)) - (ws + lax.broadcasted_iota(I32, (NSA_TQ, tw), 1))
    dist_w = dist_wi.astype(F32)
    ok_w = jnp.where(dist_wi >= 0, jnp.where(dist_wi < WINDOW, 1.0, 0.0), 0.0) > 0.5
    pws = []
    for rb in range(8):
        s = jnp.where(ok_w, sw_all[rb * NSA_TQ:(rb + 1) * NSA_TQ] - slopes[rb] * dist_w, NEG_INF)
        m = jnp.max(s, axis=-1, keepdims=True)
        p = jnp.where(ok_w, jnp.exp(s - m), 0.0)
        pw = p / jnp.maximum(jnp.sum(p, axis=-1, keepdims=True), 1e-30)
        pws.append(pw.astype(BF16))
    o_win = jnp.dot(jnp.concatenate(pws, axis=0), vwt, preferred_element_type=F32)

    gl = gl_ref[...]
    outs = []
    for rb in range(8):
        g, r = rb % 2, rb // 2
        c0 = g * NSA_GROUP * N_BRANCH + r * N_BRANCH
        rs = slice(rb * NSA_TQ, (rb + 1) * NSA_TQ)
        o_slc = acc_ref[rs] / jnp.maximum(jnp.sum(l_ref[rs], axis=-1, keepdims=True), 1e-30)
        outs.append(gl[:, c0:c0 + 1] * o_cmp[rs] + gl[:, c0 + 1:c0 + 2] * o_slc
                    + gl[:, c0 + 2:c0 + 3] * o_win[rs])
    for j in range(NSA_GROUP):
        o_ref[:, j * 128:(j + 1) * 128] = jnp.where(lo, outs[2 * j], outs[2 * j + 1]).astype(BF16)


def _nsa_prompt(qn, gl, kc, vc, ksb, vsb, kwb, vwb, batch, seq):
    nq = seq // NSA_TQ
    n_c = seq // CMP_BLOCK
    qsp = pl.BlockSpec((NSA_TQ, 512), lambda b, i: (b * nq + i, 0))
    glsp = pl.BlockSpec((NSA_TQ, 128), lambda b, i: (b * nq + i, 0))
    csp = pl.BlockSpec((1, n_c, 128), lambda b, i: (b, 0, 0))
    seqsp = pl.BlockSpec((1, seq, 128), lambda b, i: (b, 0, 0))
    r3 = lambda a: a.reshape(batch, seq, 128)
    return pl.pallas_call(
        _nsa_prompt_body,
        grid=(batch, nq),
        in_specs=[qsp, glsp, csp, csp, seqsp, seqsp, seqsp, seqsp],
        out_specs=qsp,
        out_shape=jax.ShapeDtypeStruct((batch * seq, 512), BF16),
        scratch_shapes=[pltpu.VMEM((8, NSA_TQ, LANES), BF16),
                        pltpu.VMEM((8 * NSA_TQ, LANES), F32),
                        pltpu.VMEM((8 * NSA_TQ, LANES), F32),
                        pltpu.VMEM((8 * NSA_TQ, LANES), F32),
                        pltpu.SMEM((LANES * SLC_BLOCK // NSA_TK,), I32)],
        compiler_params=_params(("parallel", "arbitrary")),
        name="nsa_prompt",
    )(qn, gl, kc, vc, r3(ksb), r3(vsb), r3(kwb), r3(vwb))


CS_CHUNK = 256


def _cumsum_body(gl_ref, ccol_ref, crow_ref, carry_ref):
    @pl.when(pl.program_id(1) == 0)
    def _():
        carry_ref[...] = jnp.zeros(carry_ref.shape, F32)

    r = lax.broadcasted_iota(I32, (CS_CHUNK, CS_CHUNK), 0)
    c = lax.broadcasted_iota(I32, (CS_CHUNK, CS_CHUNK), 1)
    tri = jnp.where(c <= r, 1.0, 0.0).astype(BF16)
    h, m, l = _split3(gl_ref[...])
    cs = (jnp.dot(tri, h, preferred_element_type=F32) + jnp.dot(tri, m, preferred_element_type=F32)
          + jnp.dot(tri, l, preferred_element_type=F32)) + carry_ref[0:1, :]
    ccol_ref[...] = cs
    crow_ref[0] = cs.T[GL_LOGF:GL_LOGF + FOX_HEADS, :]
    carry_ref[...] = jnp.broadcast_to(cs[CS_CHUNK - 1:CS_CHUNK, :], carry_ref.shape)


def _logf_cumsum(gl, batch, seq):
    nch = seq // CS_CHUNK
    return pl.pallas_call(
        _cumsum_body,
        grid=(batch, nch),
        in_specs=[pl.BlockSpec((CS_CHUNK, 128), lambda b, i: (b * nch + i, 0))],
        out_specs=[pl.BlockSpec((CS_CHUNK, 128), lambda b, i: (b * nch + i, 0)),
                   pl.BlockSpec((1, FOX_HEADS, CS_CHUNK), lambda b, i: (b, 0, i))],
        out_shape=[jax.ShapeDtypeStruct((batch * seq, 128), F32),
                   jax.ShapeDtypeStruct((batch, FOX_HEADS, seq), F32)],
        scratch_shapes=[pltpu.VMEM((SUBLANES, 128), F32)],
        compiler_params=_params(("parallel", "arbitrary")),
        name="logf_cumsum",
    )(gl)


FOX_T = 512
FOX_TW = 1024


def _fox_prompt_body(q_ref, k_ref, v_ref, ccol_ref, crow_ref, o_ref, m_ref, l_ref, acc_ref):
    qi = pl.program_id(1)
    lane = lax.broadcasted_iota(I32, (FOX_T, LANES), 1)
    lo = lane < HEAD_DIM
    zero_b = jnp.zeros((FOX_T, LANES), BF16)
    d0 = lax.broadcasted_iota(I32, (FOX_T, LANES), 0) - lane
    ccol = ccol_ref[...]
    for hp in range(FOX_HEADS // 2):
        q = q_ref[:, hp * 128:(hp + 1) * 128]
        qs = jnp.concatenate([jnp.where(lo, q, zero_b), jnp.where(lo, zero_b, q)], axis=0)
        cqs = [jnp.broadcast_to(ccol[:, GL_LOGF + 2 * hp + e:GL_LOGF + 2 * hp + e + 1], (FOX_T, LANES))
               for e in range(2)]
        m_ref[...] = jnp.full(m_ref.shape, NEG_INF, F32)
        l_ref[...] = jnp.zeros(l_ref.shape, F32)
        acc_ref[...] = jnp.zeros(acc_ref.shape, F32)

        def tile(k0, width, masked):
            kt = k_ref[pl.ds(k0, width), hp * 128:(hp + 1) * 128]
            vt = v_ref[pl.ds(k0, width), hp * 128:(hp + 1) * 128]
            s_all = lax.dot_general(qs, kt, NT_DIMS, preferred_element_type=F32)
            n_chunks = width // LANES
            for e in range(2):
                rs = slice(e * FOX_T, (e + 1) * FOX_T)
                ck = crow_ref[0, 2 * hp + e:2 * hp + e + 1, pl.ds(k0, width)]
                cols = [(s_all[rs, c * LANES:(c + 1) * LANES] + cqs[e]) - ck[:, c * LANES:(c + 1) * LANES]
                        for c in range(n_chunks)]
                if masked:
                    vis = [d0 >= c * LANES for c in range(n_chunks)]
                    cols = [jnp.where(vis[c], cols[c], NEG_INF) for c in range(n_chunks)]
                mc = cols[0]
                for c in range(1, n_chunks):
                    mc = jnp.maximum(mc, cols[c])
                m_old = m_ref[rs]
                m_new = jnp.maximum(m_old, jnp.max(mc, axis=-1, keepdims=True))
                alpha = jnp.exp(m_old - m_new)
                ps = [jnp.exp(col - m_new) for col in cols]
                if masked:
                    ps = [jnp.where(vis[c], ps[c], 0.0) for c in range(n_chunks)]
                psum = ps[0]
                for c in range(1, n_chunks):
                    psum = psum + ps[c]
                l_ref[rs] = alpha * l_ref[rs] + psum
                pb = jnp.concatenate([p.astype(BF16) for p in ps], axis=1)
                acc_ref[rs] = alpha * acc_ref[rs] + jnp.dot(pb, vt, preferred_element_type=F32)
                m_ref[rs] = m_new

        def wide_tile(t, carry):
            tile(pl.multiple_of(t * FOX_TW, FOX_TW), FOX_TW, False)
            return carry

        lax.fori_loop(0, (qi * FOX_T) // FOX_TW, wide_tile, 0)

        @pl.when(qi % (FOX_TW // FOX_T) == 1)
        def _():
            tile(pl.multiple_of((qi - 1) * FOX_T, FOX_T), FOX_T, False)

        tile(pl.multiple_of(qi * FOX_T, FOX_T), FOX_T, True)
        o = acc_ref[...] / jnp.sum(l_ref[...], axis=-1, keepdims=True)
        o_ref[:, hp * 128:(hp + 1) * 128] = jnp.where(lo, o[:FOX_T], o[FOX_T:]).astype(BF16)


def _fox_prompt(qf, kfb, vfb, ccol, crow, batch, seq):
    nq = seq // FOX_T
    qsp = pl.BlockSpec((FOX_T, 512), lambda b, i: (b * nq + i, 0))
    kvsp = pl.BlockSpec((seq, 512), lambda b, i: (b, 0))
    return pl.pallas_call(
        _fox_prompt_body,
        grid=(batch, nq),
        in_specs=[qsp, kvsp, kvsp,
                  pl.BlockSpec((FOX_T, 128), lambda b, i: (b * nq + i, 0)),
                  pl.BlockSpec((1, FOX_HEADS, seq), lambda b, i: (b, 0, 0))],
        out_specs=qsp,
        out_shape=jax.ShapeDtypeStruct((batch * seq, 512), BF16),
        scratch_shapes=[pltpu.VMEM((2 * FOX_T, LANES), F32),
                        pltpu.VMEM((2 * FOX_T, LANES), F32),
                        pltpu.VMEM((2 * FOX_T, LANES), F32)],
        compiler_params=_params(("parallel", "arbitrary")),
        name="fox_prompt",
    )(qf, kfb, vfb, ccol, crow)


def _outproj_body(on_ref, of_ref, x_ref, g0_ref, b0_ref, wn_ref, wf_ref, g1_ref, b1_ref, y_ref):
    xn = _layer_norm(x_ref[...], g0_ref[...], b0_ref[...])
    a = (jnp.dot(on_ref[...], wn_ref[...], preferred_element_type=F32)
         + jnp.dot(of_ref[...], wf_ref[...], preferred_element_type=F32))
    y_ref[...] = _layer_norm(DN_ALPHA * xn + a, g1_ref[...], b1_ref[...])


def _outproj_ln1(o_n, o_f, x2, g0, b0, w_n, w_f, g1, b1, tm):
    n = x2.shape[0]
    full = lambda a: pl.BlockSpec(a.shape, lambda i: (0,) * a.ndim)
    row = lambda w: pl.BlockSpec((tm, w), lambda i: (i, 0))
    return pl.pallas_call(
        _outproj_body,
        grid=(n // tm,),
        in_specs=[row(512), row(512), row(D_MODEL), full(g0), full(b0), full(w_n), full(w_f), full(g1), full(b1)],
        out_specs=row(D_MODEL),
        out_shape=jax.ShapeDtypeStruct((n, D_MODEL), F32),
        compiler_params=_params(("parallel",)),
        name="outproj_ln1",
    )(o_n, o_f, x2, g0, b0, w_n, w_f, g1, b1)


def _first_max(v, idx_f, big, axes):
    m = v
    for ax in axes:
        m = jnp.max(m, axis=ax, keepdims=True)
    i = jnp.where(v == m, idx_f, big)
    for ax in axes:
        i = jnp.min(i, axis=ax, keepdims=True)
    return m, i


def _router_body(x_ref, wr_ref, br_ref, e4_ref, p4_ref, w4_ref, cnt_ref, carry_ref):
    tm = x_ref.shape[0]

    @pl.when(pl.program_id(0) == 0)
    def _():
        carry_ref[...] = jnp.zeros(carry_ref.shape, F32)

    xb = x_ref[...].astype(BF16)
    logits = lax.dot_general(wr_ref[...], xb, NT_DIMS, preferred_element_type=F32)
    s = jax.nn.sigmoid(logits)
    sb = s + br_ref[...]
    ng, ne = N_EXPERT_GROUPS, EXPERTS_PER_GROUP
    s3 = s.reshape(ng, ne, tm)
    sb3 = sb.reshape(ng, ne, tm)
    e_idx = lax.broadcasted_iota(I32, (ng, ne, tm), 1).astype(F32)
    g_idx3 = lax.broadcasted_iota(I32, (ng, ne, tm), 0).astype(F32)
    flat = g_idx3 * ne + e_idx
    m1, i1 = _first_max(sb3, e_idx, float(ne), (1,))
    m2, _ = _first_max(jnp.where(e_idx == i1, -jnp.inf, sb3), e_idx, float(ne), (1,))
    gs = (m1 + m2)
    g_idx = lax.broadcasted_iota(I32, (ng, 1, tm), 0).astype(F32)
    keep = jnp.zeros((ng, 1, tm), F32)
    for _ in range(TOPK_GROUPS):
        _, gi = _first_max(gs, g_idx, float(ng), (0,))
        hit = g_idx == gi
        keep = jnp.where(hit, 1.0, keep)
        gs = jnp.where(hit, -jnp.inf, gs)
    cand = jnp.where(keep > 0.5, sb3, NEG_INF)
    hits, es, ws = [], [], []
    for _ in range(TOP_K):
        _, fi = _first_max(cand, flat, float(N_EXPERTS), (1, 0))
        hit = flat == fi
        cand = jnp.where(hit, -jnp.inf, cand)
        hits.append(jnp.where(hit, 1.0, 0.0))
        es.append(fi.reshape(1, tm))
        ws.append(jnp.sum(jnp.sum(jnp.where(hit, s3, 0.0), axis=1, keepdims=True), axis=0, keepdims=True).reshape(1, tm))
    wsum = ((ws[0] + ws[1]) + ws[2]) + ws[3]
    sel = (hits[0] + hits[1] + hits[2] + hits[3]).reshape(N_EXPERTS, tm)
    r = lax.broadcasted_iota(I32, (tm, tm), 0)
    c = lax.broadcasted_iota(I32, (tm, tm), 1)
    strict = jnp.where(r < c, 1.0, 0.0).astype(BF16)
    pos = jnp.dot(sel.astype(BF16), strict, preferred_element_type=F32) + carry_ref[:, 0:1]
    pos3 = pos.reshape(ng, ne, tm)
    for k in range(TOP_K):
        e4_ref[k:k + 1, :] = es[k].astype(I32)
        pk = jnp.sum(jnp.sum(hits[k] * pos3, axis=1, keepdims=True), axis=0, keepdims=True).reshape(1, tm)
        p4_ref[k:k + 1, :] = pk.astype(I32)
        w4_ref[k:k + 1, :] = ROUTED_SCALE * ws[k] / wsum
    total = carry_ref[...] + jnp.sum(sel, axis=1, keepdims=True)
    carry_ref[...] = total
    cnt_ref[...] = total.astype(I32)


def _router(x1, wr_t, br_col, tm):
    n = x1.shape[0]
    osp = pl.BlockSpec((TOP_K, tm), lambda i: (0, i))
    return pl.pallas_call(
        _router_body,
        grid=(n // tm,),
        in_specs=[pl.BlockSpec((tm, D_MODEL), lambda i: (i, 0)),
                  pl.BlockSpec(wr_t.shape, lambda i: (0, 0)),
                  pl.BlockSpec(br_col.shape, lambda i: (0, 0))],
        out_specs=[osp, osp, osp, pl.BlockSpec((N_EXPERTS, LANES), lambda i: (0, 0))],
        out_shape=[jax.ShapeDtypeStruct((TOP_K, n), I32), jax.ShapeDtypeStruct((TOP_K, n), I32),
                   jax.ShapeDtypeStruct((TOP_K, n), F32), jax.ShapeDtypeStruct((N_EXPERTS, LANES), I32)],
        scratch_shapes=[pltpu.VMEM((N_EXPERTS, LANES), F32)],
        compiler_params=_params(("arbitrary",)),
        name="moe_router",
    )(x1, wr_t, br_col)


def _row_copy(src, s, dst, d, sem):
    return pltpu.make_async_copy(src.at[pl.ds(s, 1)], dst.at[pl.ds(d, 1)], sem)


def _dispatch_body(dest_ref, x_ref, xb_in_ref, xb_ref, sem):
    del xb_in_ref
    tm = dest_ref.shape[1]

    def start(t, carry):
        for k in range(TOP_K):
            _row_copy(x_ref, t, xb_ref, dest_ref[k, t], sem).start()
        return carry

    lax.fori_loop(0, tm, start, 0)

    def wait(t, carry):
        for k in range(TOP_K):
            _row_copy(x_ref, 0, xb_ref, 0, sem).wait()
        return carry

    lax.fori_loop(0, tm, wait, 0)


def _dispatch(dest4, x1, n_rows, tm):
    n = x1.shape[0]
    xb0 = jnp.zeros((n_rows, D_MODEL), F32)
    return pl.pallas_call(
        _dispatch_body,
        grid=(n // tm,),
        in_specs=[pl.BlockSpec((TOP_K, tm), lambda i: (0, i), memory_space=pltpu.SMEM),
                  pl.BlockSpec((tm, D_MODEL), lambda i: (i, 0)),
                  pl.BlockSpec(memory_space=pl.ANY)],
        out_specs=pl.BlockSpec(memory_space=pl.ANY),
        out_shape=jax.ShapeDtypeStruct((n_rows, D_MODEL), F32),
        scratch_shapes=[pltpu.SemaphoreType.DMA(())],
        input_output_aliases={2: 0},
        compiler_params=pltpu.CompilerParams(dimension_semantics=("arbitrary",), has_side_effects=True,
                                             vmem_limit_bytes=VMEM_LIMIT),
        name="moe_dispatch",
    )(dest4, x1, xb0)


def _expert_body(blk_e_ref, n_used_ref, x_ref, wg_ref, wu_ref, wd_ref, y_ref):
    del blk_e_ref
    i = pl.program_id(0)

    @pl.when(i < n_used_ref[0])
    def _():
        xs = x_ref[...].astype(BF16)
        hg = jnp.dot(xs, wg_ref[0].astype(BF16), preferred_element_type=F32)
        hu = jnp.dot(xs, wu_ref[0].astype(BF16), preferred_element_type=F32)
        h = (jax.nn.silu(hg) * hu).astype(BF16)
        y_ref[...] = jnp.dot(h, wd_ref[0].astype(BF16), preferred_element_type=F32)

    @pl.when(i >= n_used_ref[0])
    def _():
        y_ref[...] = jnp.zeros(y_ref.shape, F32)


def _experts(blk_e, n_used, xb, w_g, w_u, w_d, blk):
    n_rows = xb.shape[0]
    grid_spec = pltpu.PrefetchScalarGridSpec(
        num_scalar_prefetch=2,
        grid=(n_rows // blk,),
        in_specs=[pl.BlockSpec((blk, D_MODEL), lambda i, be, nu: (i, 0)),
                  pl.BlockSpec((1, D_MODEL, D_EXPERT), lambda i, be, nu: (be[i], 0, 0)),
                  pl.BlockSpec((1, D_MODEL, D_EXPERT), lambda i, be, nu: (be[i], 0, 0)),
                  pl.BlockSpec((1, D_EXPERT, D_MODEL), lambda i, be, nu: (be[i], 0, 0))],
        out_specs=pl.BlockSpec((blk, D_MODEL), lambda i, be, nu: (i, 0)),
    )
    return pl.pallas_call(
        _expert_body,
        grid_spec=grid_spec,
        out_shape=jax.ShapeDtypeStruct((n_rows, D_MODEL), F32),
        compiler_params=_params(("arbitrary",)),
        name="moe_experts",
    )(blk_e, n_used, xb, w_g, w_u, w_d)


def _combine_body(dest_ref, w_ref, x_ref, yb_ref, wsg_ref, wsu_ref, wsd_ref, g_ref, b_ref, y_ref, gbuf, sem):
    tm = x_ref.shape[0]

    def start(t, carry):
        for k in range(TOP_K):
            pltpu.make_async_copy(yb_ref.at[pl.ds(dest_ref[k, t], 1)], gbuf.at[k, pl.ds(t, 1)], sem).start()
        return carry

    lax.fori_loop(0, tm, start, 0)
    x = x_ref[...]
    xb = x.astype(BF16)
    hs = jax.nn.silu(jnp.dot(xb, wsg_ref[...], preferred_element_type=F32)) * jnp.dot(xb, wsu_ref[...], preferred_element_type=F32)
    shared = jnp.dot(hs.astype(BF16), wsd_ref[...], preferred_element_type=F32)

    def wait(t, carry):
        for k in range(TOP_K):
            pltpu.make_async_copy(yb_ref.at[pl.ds(0, 1)], gbuf.at[k, pl.ds(0, 1)], sem).wait()
        return carry

    lax.fori_loop(0, tm, wait, 0)
    w = w_ref[...]
    routed = gbuf[0] * w[:, 0:1]
    for k in range(1, TOP_K):
        routed = routed + gbuf[k] * w[:, k:k + 1]
    y_ref[...] = _layer_norm(DN_ALPHA * x + (routed + shared), g_ref[...], b_ref[...])


def _combine(dest4, w4t, x1, yb, wsg, wsu, wsd, g2, b2, tm):
    n = x1.shape[0]
    full = lambda a: pl.BlockSpec(a.shape, lambda i: (0,) * a.ndim)
    return pl.pallas_call(
        _combine_body,
        grid=(n // tm,),
        in_specs=[pl.BlockSpec((TOP_K, tm), lambda i: (0, i), memory_space=pltpu.SMEM),
                  pl.BlockSpec((tm, TOP_K), lambda i: (i, 0)),
                  pl.BlockSpec((tm, D_MODEL), lambda i: (i, 0)),
                  pl.BlockSpec(memory_space=pl.ANY),
                  full(wsg), full(wsu), full(wsd), full(g2), full(b2)],
        out_specs=pl.BlockSpec((tm, D_MODEL), lambda i: (i, 0)),
        out_shape=jax.ShapeDtypeStruct((n, D_MODEL), F32),
        scratch_shapes=[pltpu.VMEM((TOP_K, tm, D_MODEL), F32), pltpu.SemaphoreType.DMA(())],
        compiler_params=_params(("arbitrary",)),
        name="moe_combine",
    )(dest4, w4t, x1, yb, wsg, wsu, wsd, g2, b2)


def _moe_ln2(x1, wr_t, br_col, w_g, w_u, w_d, wsg, wsu, wsd, g2, b2, tm, blk):
    n = x1.shape[0]
    e4, p4, w4, cnt = _router(x1, wr_t, br_col, min(tm, 128))
    counts = cnt[:, 0]
    padded = (counts + blk - 1) // blk * blk
    pend = jnp.cumsum(padded)
    pstart = pend - padded
    e_ids = jnp.arange(N_EXPERTS, dtype=I32)
    dest4 = jnp.sum(jnp.where(e4[..., None] == e_ids, pstart.astype(I32), 0), axis=-1) + p4
    n_blocks = -(-(n * TOP_K + N_EXPERTS * (blk - 1)) // blk)
    starts = jnp.arange(n_blocks, dtype=I32) * blk
    blk_e = jnp.minimum(jnp.sum((pend[None, :] <= starts[:, None]).astype(I32), axis=1), N_EXPERTS - 1)
    n_used = (pend[-1:] // blk).astype(I32)
    xb = _dispatch(dest4, x1, n_blocks * blk, tm)
    yb = _experts(blk_e, n_used, xb, w_g, w_u, w_d, blk)
    return _combine(dest4, w4.T, x1, yb, wsg, wsu, wsd, g2, b2, tm)


PC_SLABS = 256


def _paged_compress_body(x_ref, w_ref, o_ref):
    acc = jnp.zeros((PC_SLABS, 4 * HEAD_DIM), F32)
    for dp in range(HEAD_DIM // 2):
        a = x_ref[pl.ds(2 * dp, PC_SLABS, stride=HEAD_DIM), :]
        b = x_ref[pl.ds(2 * dp + 1, PC_SLABS, stride=HEAD_DIM), :]
        lhs = jnp.concatenate([a, b], axis=1).astype(BF16)
        acc = acc + jnp.dot(lhs, w_ref[dp], preferred_element_type=F32)
    o_ref[...] = acc


def _paged_compress(cache_t, w_big):
    n_slabs = cache_t.shape[0] * cache_t.shape[1]
    x = cache_t.reshape(n_slabs * HEAD_DIM, PAGE_SIZE)
    return pl.pallas_call(
        _paged_compress_body,
        grid=(n_slabs // PC_SLABS,),
        in_specs=[pl.BlockSpec((PC_SLABS * HEAD_DIM, PAGE_SIZE), lambda i: (i, 0)),
                  pl.BlockSpec(w_big.shape, lambda i: (0, 0, 0))],
        out_specs=pl.BlockSpec((PC_SLABS, 4 * HEAD_DIM), lambda i: (i, 0)),
        out_shape=jax.ShapeDtypeStruct((n_slabs, 4 * HEAD_DIM), F32),
        compiler_params=_params(("parallel",)),
        name="paged_compress",
    )(x, w_big)


def _paged_cmp_weight(w):
    eye = jnp.eye(PAGE_SIZE // CMP_BLOCK, dtype=w.dtype)
    wb = jnp.einsum('ide,nm->dnime', w, eye)
    return wb.reshape(HEAD_DIM // 2, 2 * PAGE_SIZE, 4 * HEAD_DIM).astype(BF16)


QROWS = 16


def _head_slopes(shape, axis):
    h = lax.broadcasted_iota(I32, shape, axis)
    s = jnp.zeros(shape, F32)
    for i in range(NSA_HEADS):
        s = jnp.where(h == i, float(2.0 ** (-(i + 1))), s)
    return s


SEL_NB = SUBLANES // NSA_KV_HEADS


def _nsa_dec_select_body(q_ref, kc_ref, vc_ref, sel_ref, oc_ref, *, past):
    n_c = kc_ref.shape[1]
    half = n_c // 2
    col = lax.broadcasted_iota(I32, (QROWS, n_c), 1)
    cblk = 2 * (col % half) + col // half
    dist = (past - (cblk * CMP_BLOCK + (CMP_BLOCK - 1))).astype(F32)
    mask = dist >= 0
    bias = _head_slopes((QROWS, n_c), 0) * dist
    row = lax.broadcasted_iota(I32, (QROWS, n_c), 0)
    lane = lax.broadcasted_iota(I32, (SUBLANES, LANES), 1)
    srow = lax.broadcasted_iota(I32, (SUBLANES, LANES), 0)
    lane_f = lane.astype(F32)
    lane_o = lax.broadcasted_iota(I32, (NSA_GROUP, LANES), 1)
    forced = (lane[0:1] == 0) | (lane[0:1] == past // SLC_BLOCK)
    score = jnp.full((SUBLANES, LANES), -jnp.inf, F32)
    for i in range(SEL_NB):
        lg = lax.dot_general(q_ref[i], kc_ref[i], NT_DIMS, preferred_element_type=F32)
        lc = jnp.where(mask, lg - bias, NEG_INF)
        m = jnp.max(lc, axis=-1, keepdims=True)
        p = jnp.where(mask, jnp.exp(lc - m), 0.0)
        pc = p / jnp.maximum(jnp.sum(p, axis=-1, keepdims=True), 1e-30)
        o = jnp.dot(pc.astype(BF16), vc_ref[i], preferred_element_type=F32)
        oc_ref[i] = jnp.where(lane_o < HEAD_DIM, o[:NSA_GROUP], o[NSA_GROUP:NSA_HEADS])
        for g in range(NSA_KV_HEADS):
            in_g = (row >= g * NSA_GROUP) & (row < (g + 1) * NSA_GROUP)
            sg = jnp.sum(jnp.where(in_g, pc, 0.0), axis=0, keepdims=True)
            imp = sg[:, :half] + sg[:, half:]
            sc = jnp.where(lane[0:1] * SLC_BLOCK <= past, imp + jnp.where(forced, FORCE_BONUS, 0.0), -1.0)
            score = jnp.where(srow == NSA_KV_HEADS * i + g, jnp.broadcast_to(sc, (SUBLANES, LANES)), score)
    n_before = jnp.sum(jnp.where(score >= FORCE_BONUS, 1.0, 0.0), axis=-1, keepdims=True)
    new_sel = n_before < float(SLC_TOPK)
    order = jnp.zeros((SUBLANES, LANES), F32)
    s = score
    for k in range(SLC_TOPK):
        mk = jnp.max(s, axis=-1, keepdims=True)
        idx = jnp.min(jnp.where(s == mk, lane_f, float(LANES)), axis=-1, keepdims=True)
        order = jnp.where(lane == k, idx, order)
        s = jnp.where(lane_f == idx, -jnp.inf, s)
    order = jnp.where((lane == SLC_TOPK - 1) & new_sel, float(LANES), order)
    sel_ref[0] = order.astype(I32)


def _nsa_dec_select(qmat, kc, vc, past):
    bd, n_c = kc.shape[0], kc.shape[1]
    return pl.pallas_call(
        functools.partial(_nsa_dec_select_body, past=past),
        grid=(bd // SEL_NB,),
        in_specs=[pl.BlockSpec((SEL_NB, QROWS, LANES), lambda b: (b, 0, 0)),
                  pl.BlockSpec((SEL_NB, n_c, LANES), lambda b: (b, 0, 0)),
                  pl.BlockSpec((SEL_NB, n_c, LANES), lambda b: (b, 0, 0))],
        out_specs=[pl.BlockSpec((1, SUBLANES, LANES), lambda b: (b, 0, 0)),
                   pl.BlockSpec((SEL_NB, NSA_GROUP, LANES), lambda b: (b, 0, 0))],
        out_shape=[jax.ShapeDtypeStruct((bd // SEL_NB, SUBLANES, LANES), I32),
                   jax.ShapeDtypeStruct((bd, NSA_GROUP, LANES), F32)],
        compiler_params=_params(("parallel",)),
        name="nsa_decode_select",
    )(qmat, kc, vc)


ND_SLABS = 16
ND_STEPS = SLC_TOPK // ND_SLABS
ND_PAGE, ND_HALF, ND_SLOT, ND_NEW = 0, 32, 64, 96


def _nsa_dec_attend_body(meta_ref, *refs, past):
    ks_refs, vs_refs = refs[:ND_SLABS], refs[ND_SLABS:2 * ND_SLABS]
    (kw_ref, vw_ref, q_ref, nks_ref, nvs_ref, nkw_ref, nvw_ref, o_ref, m_ref, l_ref, acc_ref) = refs[2 * ND_SLABS:]
    bg = pl.program_id(0)
    kh = pl.program_id(1)
    b = bg // NSA_KV_HEADS
    g = bg % NSA_KV_HEADS
    lane = lax.broadcasted_iota(I32, (1, LANES), 1)
    qb = [jnp.broadcast_to(q_ref[0, r], (HEAD_DIM, LANES)) for r in range(NSA_GROUP)]
    slopes = [jnp.where(g == 0, _slope(0, r), _slope(1, r)) for r in range(NSA_GROUP)]

    @pl.when(kh == 0)
    def _():
        m_ref[...] = jnp.full(m_ref.shape, NEG_INF, F32)
        l_ref[...] = jnp.zeros(l_ref.shape, F32)
        acc_ref[...] = jnp.zeros(acc_ref.shape, F32)

    rows = [[] for _ in range(NSA_GROUP)]
    for i in range(ND_SLABS):
        j = g * SLC_TOPK + kh * ND_SLABS + i
        ok = (lane // SLC_BLOCK) == meta_ref[b, ND_HALF + j]
        dist = (past - (meta_ref[b, ND_SLOT + j] * PAGE_SIZE + lane)).astype(F32)
        kt = ks_refs[i][0, 0]
        for r in range(NSA_GROUP):
            rowv = jnp.sum(qb[r] * kt, axis=0, keepdims=True)
            rows[r].append(jnp.where(ok, rowv - slopes[r] * dist, NEG_INF))
    for r in range(NSA_GROUP):
        s = jnp.concatenate(rows[r], axis=0)
        m_old = m_ref[r:r + 1, :]
        m_new = jnp.maximum(m_old, jnp.max(jnp.max(s, axis=-1, keepdims=True), axis=0, keepdims=True))
        alpha = jnp.exp(m_old - m_new)
        p = jnp.where(s > 0.5 * NEG_INF, jnp.exp(s - m_new), 0.0)
        l_ref[r:r + 1, :] = alpha * l_ref[r:r + 1, :] + jnp.sum(jnp.sum(p, axis=-1, keepdims=True), axis=0, keepdims=True)
        acc = alpha * acc_ref[r]
        for i in range(ND_SLABS):
            acc = acc + p[i:i + 1, :] * vs_refs[i][0, 0]
        acc_ref[r] = acc
        m_ref[r:r + 1, :] = m_new

    @pl.when(kh == ND_STEPS - 1)
    def _():
        lane_o = lax.broadcasted_iota(I32, (HEAD_DIM, LANES), 1)
        out = jnp.zeros((HEAD_DIM, LANES), F32)
        new_on = meta_ref[b, ND_NEW + g] > 0
        n_wt = WINDOW // LANES
        wrows = [[] for _ in range(NSA_GROUP)]
        for c in range(n_wt):
            kt = kw_ref[0, :, c * LANES:(c + 1) * LANES]
            dist_i = WINDOW - (c * LANES + lane)
            for r in range(NSA_GROUP):
                rowv = jnp.sum(qb[r] * kt, axis=0, keepdims=True)
                wrows[r].append(jnp.where(dist_i < WINDOW, rowv - slopes[r] * dist_i.astype(F32), NEG_INF))
        for r in range(NSA_GROUP):
            lgn = jnp.where(new_on, jnp.sum(q_ref[0, r] * nks_ref[0], axis=0, keepdims=True), NEG_INF)
            m_old = m_ref[r:r + 1, :]
            m_fin = jnp.maximum(m_old, lgn)
            alpha = jnp.exp(m_old - m_fin)
            pn = jnp.where(new_on, jnp.exp(lgn - m_fin), 0.0)
            den = alpha * l_ref[r:r + 1, :] + pn
            o_col = (alpha * jnp.sum(acc_ref[r], axis=-1, keepdims=True) + pn * nvs_ref[0]) / jnp.maximum(den, 1e-30)
            out = jnp.where(lane_o == 2 * r, jnp.broadcast_to(o_col, (HEAD_DIM, LANES)), out)
            lgw = jnp.sum(q_ref[0, r] * nkw_ref[0], axis=0, keepdims=True)
            lw = jnp.concatenate(wrows[r], axis=0)
            m = jnp.maximum(jnp.max(jnp.max(lw, axis=-1, keepdims=True), axis=0, keepdims=True), lgw)
            p = jnp.where(lw > 0.5 * NEG_INF, jnp.exp(lw - m), 0.0)
            pw = jnp.exp(lgw - m)
            den = jnp.sum(jnp.sum(p, axis=-1, keepdims=True), axis=0, keepdims=True) + pw
            acc = jnp.zeros((HEAD_DIM, LANES), F32)
            for c in range(n_wt):
                acc = acc + p[c:c + 1, :] * vw_ref[0, :, c * LANES:(c + 1) * LANES]
            o_col = (jnp.sum(acc, axis=-1, keepdims=True) + pw * nvw_ref[0]) / jnp.maximum(den, 1e-30)
            out = jnp.where(lane_o == 2 * r + 1, jnp.broadcast_to(o_col, (HEAD_DIM, LANES)), out)
        o_ref[0] = out


def _nsa_dec_meta(page_table, sel, past):
    bd = sel.shape[0]
    n_past_blk = past // SLC_BLOCK
    is_new = sel >= n_past_blk
    sp = jnp.minimum(sel, n_past_blk - 1)
    slot = sp // (PAGE_SIZE // SLC_BLOCK)
    page = jnp.take_along_axis(page_table, slot, axis=1)
    half = jnp.where(is_new, 2, sp % (PAGE_SIZE // SLC_BLOCK))
    has_new = jnp.any(is_new.reshape(bd, NSA_KV_HEADS, SLC_TOPK), axis=-1).astype(I32)
    pad = jnp.zeros((bd, LANES - ND_NEW - NSA_KV_HEADS), I32)
    return jnp.concatenate([page, half, slot, has_new, pad], axis=1).astype(I32)


def _nsa_dec_attend(meta, slc_k_t, slc_v_t, win_k_t, win_v_t, qcol, nks, nvs, nkw, nvw, past):
    n_bg = qcol.shape[0]

    def slab_spec(i):
        def imap(bg, kh, mt):
            g = bg % NSA_KV_HEADS
            return (mt[bg // NSA_KV_HEADS, ND_PAGE + g * SLC_TOPK + kh * ND_SLABS + i], g, 0, 0)
        return pl.BlockSpec((1, 1, HEAD_DIM, PAGE_SIZE), imap)

    slabs = [slab_spec(i) for i in range(ND_SLABS)]
    wsp = pl.BlockSpec((1, HEAD_DIM, WINDOW), lambda bg, kh, mt: (bg, 0, 0))
    csp = pl.BlockSpec((1, HEAD_DIM, 1), lambda bg, kh, mt: (bg, 0, 0))
    grid_spec = pltpu.PrefetchScalarGridSpec(
        num_scalar_prefetch=1,
        grid=(n_bg, ND_STEPS),
        in_specs=slabs + slabs + [wsp, wsp, pl.BlockSpec((1, NSA_GROUP, HEAD_DIM, 1), lambda bg, kh, mt: (bg, 0, 0, 0)),
                                  csp, csp, csp, csp],
        out_specs=pl.BlockSpec((1, HEAD_DIM, LANES), lambda bg, kh, mt: (bg, 0, 0)),
        scratch_shapes=[pltpu.VMEM((NSA_GROUP, 1), F32),
                        pltpu.VMEM((NSA_GROUP, 1), F32),
                        pltpu.VMEM((NSA_GROUP, HEAD_DIM, LANES), F32)],
    )
    return pl.pallas_call(
        functools.partial(_nsa_dec_attend_body, past=past),
        grid_spec=grid_spec,
        out_shape=jax.ShapeDtypeStruct((n_bg, HEAD_DIM, LANES), F32),
        compiler_params=_params(("parallel", "arbitrary")),
        name="nsa_decode_attend",
    )(meta, *([slc_k_t] * ND_SLABS), *([slc_v_t] * ND_SLABS), win_k_t, win_v_t, qcol, nks, nvs, nkw, nvw)


FD_PPS = 16


def _fox_dec_body(pt_ref, *refs):
    del pt_ref
    k_refs, v_refs, lf_refs = refs[:FD_PPS], refs[FD_PPS:2 * FD_PPS], refs[2 * FD_PPS:3 * FD_PPS]
    q_ref, nk_ref, nv_ref, nlf_ref, o_ref, qb_ref, m_ref, l_ref, acc_ref, carry_ref = refs[3 * FD_PPS:]
    j = pl.program_id(1)
    nh = FOX_HEADS
    lane3 = lax.broadcasted_iota(I32, (nh, HEAD_DIM, LANES), 2)

    @pl.when(j == 0)
    def _():
        q = q_ref[0]
        qb_ref[...] = jnp.broadcast_to(q, (nh, HEAD_DIM, LANES))
        m_ref[...] = jnp.sum(q * nk_ref[0], axis=1)
        l_ref[...] = jnp.ones(l_ref.shape, F32)
        acc_ref[...] = jnp.where(lane3 == 0, jnp.broadcast_to(nv_ref[0], (nh, HEAD_DIM, LANES)), 0.0)
        carry_ref[...] = nlf_ref[0]

    r = lax.broadcasted_iota(I32, (PAGE_SIZE, PAGE_SIZE), 0)
    c = lax.broadcasted_iota(I32, (PAGE_SIZE, PAGE_SIZE), 1)
    later = jnp.where(r > c, 1.0, 0.0).astype(BF16)
    for i in range(FD_PPS):
        lf = lf_refs[i][0, 0]
        h3, m3, l3 = _split3(lf)
        sfx = (jnp.dot(h3, later, preferred_element_type=F32) + jnp.dot(m3, later, preferred_element_type=F32)
               + jnp.dot(l3, later, preferred_element_type=F32))
        bias = carry_ref[...] + sfx
        carry_ref[...] = carry_ref[...] + jnp.sum(lf, axis=-1, keepdims=True)
        rows = [jnp.sum(qb_ref[h] * k_refs[i][0, 0, h], axis=0, keepdims=True) for h in range(nh)]
        s = jnp.concatenate(rows, axis=0) + bias
        m_old = m_ref[...]
        m_new = jnp.maximum(m_old, jnp.max(s, axis=-1, keepdims=True))
        alpha = jnp.exp(m_old - m_new)
        p = jnp.exp(s - m_new)
        l_ref[...] = alpha * l_ref[...] + jnp.sum(p, axis=-1, keepdims=True)
        m_ref[...] = m_new
        for h in range(nh):
            acc_ref[h] = alpha[h:h + 1, :] * acc_ref[h] + p[h:h + 1, :] * v_refs[i][0, 0, h]

    @pl.when(j == pl.num_programs(1) - 1)
    def _():
        lane_o = lax.broadcasted_iota(I32, (HEAD_DIM, LANES), 1)
        out = jnp.zeros((HEAD_DIM, LANES), F32)
        l = l_ref[...]
        for h in range(nh):
            o_col = jnp.sum(acc_ref[h], axis=-1, keepdims=True) / l[h:h + 1, :]
            out = jnp.where(lane_o == h, jnp.broadcast_to(o_col, (HEAD_DIM, LANES)), out)
        o_ref[0] = out


def _fox_decode(page_table, fox_k_t, fox_v_t, fox_lf_t, qcol, nk, nv, nlf):
    bd, n_pages = page_table.shape
    n_steps = n_pages // FD_PPS

    def page(i):
        return lambda b, j, pt: (0, pt[b, n_pages - 1 - (j * FD_PPS + i)], 0, 0, 0)

    def page4(i):
        return lambda b, j, pt: (0, pt[b, n_pages - 1 - (j * FD_PPS + i)], 0, 0)

    kv_specs = [pl.BlockSpec((1, 1, FOX_HEADS, HEAD_DIM, PAGE_SIZE), page(i)) for i in range(FD_PPS)]
    lf_specs = [pl.BlockSpec((1, 1, FOX_HEADS, PAGE_SIZE), page4(i)) for i in range(FD_PPS)]
    csp = pl.BlockSpec((1, FOX_HEADS, HEAD_DIM, 1), lambda b, j, pt: (b, 0, 0, 0))
    grid_spec = pltpu.PrefetchScalarGridSpec(
        num_scalar_prefetch=1,
        grid=(bd, n_steps),
        in_specs=kv_specs + kv_specs + lf_specs + [csp, csp, csp, pl.BlockSpec((1, FOX_HEADS, 1), lambda b, j, pt: (b, 0, 0))],
        out_specs=pl.BlockSpec((1, HEAD_DIM, LANES), lambda b, j, pt: (b, 0, 0)),
        scratch_shapes=[pltpu.VMEM((FOX_HEADS, HEAD_DIM, LANES), F32),
                        pltpu.VMEM((FOX_HEADS, 1), F32),
                        pltpu.VMEM((FOX_HEADS, 1), F32),
                        pltpu.VMEM((FOX_HEADS, HEAD_DIM, LANES), F32),
                        pltpu.VMEM((FOX_HEADS, 1), F32)],
    )
    return pl.pallas_call(
        _fox_dec_body,
        grid_spec=grid_spec,
        out_shape=jax.ShapeDtypeStruct((bd, HEAD_DIM, LANES), F32),
        compiler_params=_params(("parallel", "arbitrary")),
        name="fox_decode",
    )(page_table, *([fox_k_t] * FD_PPS), *([fox_v_t] * FD_PPS), *([fox_lf_t] * FD_PPS), qcol, nk, nv, nlf)


def _outproj_dec_body(oc_ref, os_ref, ow_ref, gc_ref, gs_ref, gw_ref, of_ref, x_ref, g0_ref, b0_ref,
                      wn_ref, wf_ref, g1_ref, b1_ref, y_ref):
    xn = _layer_norm(x_ref[...], g0_ref[...], b0_ref[...])
    o_n = gc_ref[...] * oc_ref[...] + gs_ref[...] * os_ref[...] + gw_ref[...] * ow_ref[...]
    a = (jnp.dot(o_n.astype(BF16), wn_ref[...], preferred_element_type=F32)
         + jnp.dot(of_ref[...].astype(BF16), wf_ref[...], preferred_element_type=F32))
    y_ref[...] = _layer_norm(DN_ALPHA * xn + a, g1_ref[...], b1_ref[...])


def _outproj_dec(oc, osl, ow, gc, gs, gw, o_f, x2, g0, b0, w_n, w_f, g1, b1):
    n = x2.shape[0]
    args = (oc, osl, ow, gc, gs, gw, o_f, x2, g0, b0, w_n, w_f, g1, b1)
    return pl.pallas_call(
        _outproj_dec_body,
        grid=(1,),
        in_specs=[pl.BlockSpec(a.shape, lambda i: (0,) * a.ndim) for a in args],
        out_specs=pl.BlockSpec((n, D_MODEL), lambda i: (0, 0)),
        out_shape=jax.ShapeDtypeStruct((n, D_MODEL), F32),
        compiler_params=_params(("arbitrary",)),
        name="outproj_ln1_decode",
    )(*args)


def kernel(x_prompt, x_sample, cache_fox_k, cache_fox_v, cache_fox_logf, cache_cmp_k, cache_cmp_v, cache_slc_k, cache_slc_v, state_win_k, state_win_v, page_table, ln_in_g, ln_in_b, w_in, b_in, w_cmp_k, w_cmp_v, w_out, ln1_g, ln1_b, w_router, b_router, w_exp_gate, w_exp_up, w_exp_down, w_sh_gate, w_sh_up, w_sh_down, ln2_g, ln2_b):
    batch, seq, _ = x_prompt.shape
    bd = x_sample.shape[0]
    l = 0
    g0, b0 = ln_in_g.reshape(1, D_MODEL), ln_in_b.reshape(1, D_MODEL)
    wp, bp = _prep_proj_weights(w_in[l], b_in[l])
    wck, wcv = _blockdiag_cmp_weight(w_cmp_k[l]), _blockdiag_cmp_weight(w_cmp_v[l])
    nsa_perm = _nsa_col_perm()
    w_on = jnp.concatenate([w_out[l][nsa_perm[c]:nsa_perm[c] + HEAD_DIM] for c in range(0, NSA_HEADS * HEAD_DIM, HEAD_DIM)],
                           axis=0).astype(BF16)
    w_of = w_out[l][NSA_HEADS * HEAD_DIM:].astype(BF16)
    wr_t = w_router[l].T.astype(BF16)
    br_col = b_router[l].reshape(N_EXPERTS, 1)
    wsg, wsu, wsd = w_sh_gate[l].astype(BF16), w_sh_up[l].astype(BF16), w_sh_down[l].astype(BF16)
    g1, b1, g2, b2 = ln1_g[l:l + 1], ln1_b[l:l + 1], ln2_g[l:l + 1], ln2_b[l:l + 1]

    xp2 = x_prompt.reshape(batch * seq, D_MODEL)
    (qn, qf, kfb, vfb, ksb, vsb, kwb, vwb, fk, fv, ck, cv, sk, sv, wk, wv, gl) = _project(xp2, g0, b0, wp, bp, 512)
    kc, vc = _compress_prompt(ck, cv, wck, wcv, batch, seq)
    o_n = _nsa_prompt(qn, gl, kc, vc, ksb, vsb, kwb, vwb, batch, seq)
    ccol, crow = _logf_cumsum(gl, batch, seq)
    o_f = _fox_prompt(qf, kfb, vfb, ccol, crow, batch, seq)
    x1 = _outproj_ln1(o_n, o_f, xp2, g0, b0, w_on, w_of, g1, b1, 512)
    y_p = _moe_ln2(x1, wr_t, br_col, w_exp_gate[l], w_exp_up[l], w_exp_down[l], wsg, wsu, wsd, g2, b2, 256, 256)

    st5 = lambda a, h: a.reshape(1, batch, seq, h, HEAD_DIM)
    keep_p = min(WINDOW, seq)
    p_outs = (st5(fk, FOX_HEADS), st5(fv, FOX_HEADS),
              gl[:, GL_LOGF:GL_LOGF + FOX_HEADS].reshape(1, batch, seq, FOX_HEADS),
              st5(ck, NSA_KV_HEADS), st5(cv, NSA_KV_HEADS), st5(sk, NSA_KV_HEADS), st5(sv, NSA_KV_HEADS),
              st5(wk, NSA_KV_HEADS)[:, :, seq - keep_p:], st5(wv, NSA_KV_HEADS)[:, :, seq - keep_p:])
    n_pages = page_table.shape[1]
    n_phys = cache_cmp_k.shape[1]
    past = n_pages * PAGE_SIZE
    assert x_sample.shape[1] == 1 and past // SLC_BLOCK == LANES and state_win_k.shape[2] == WINDOW
    xs2 = x_sample.reshape(bd, D_MODEL)
    (qn_s, qf_s, _, _, _, _, _, _, fk_s, fv_s, ck_s, cv_s, sk_s, sv_s, wk_s, wv_s, gl_s) = _project(xs2, g0, b0, wp, bp, bd)

    page_minor = lambda c: jnp.transpose(c[l], (0, 2, 3, 1))
    kc_phys = _paged_compress(page_minor(cache_cmp_k), _paged_cmp_weight(w_cmp_k[l]))
    vc_phys = _paged_compress(page_minor(cache_cmp_v), _paged_cmp_weight(w_cmp_v[l]))

    def seq_blocks(c_phys):
        blocks_per_page = PAGE_SIZE // CMP_BLOCK
        c = c_phys.reshape(n_phys, NSA_KV_HEADS, blocks_per_page, HEAD_DIM)[page_table]
        c = c.transpose(0, 1, 3, 2, 4).reshape(bd, n_pages * blocks_per_page // 2, 2, 128)
        return c.transpose(0, 2, 1, 3).reshape(bd, n_pages * blocks_per_page, 128).astype(BF16)

    q4 = qn_s.reshape(bd, NSA_GROUP, NSA_KV_HEADS, HEAD_DIM).transpose(0, 2, 1, 3)
    zq = jnp.zeros_like(q4[:, 0])
    qmat = jnp.concatenate([jnp.concatenate([q4[:, 0], zq], axis=-1), jnp.concatenate([zq, q4[:, 1]], axis=-1)], axis=1)
    qmat = jnp.pad(qmat, ((0, 0), (0, QROWS - NSA_HEADS), (0, 0)))
    sel, o_cmp = _nsa_dec_select(qmat, seq_blocks(kc_phys), seq_blocks(vc_phys), past)
    sel2 = sel[:, :, :SLC_TOPK].reshape(bd, NSA_KV_HEADS * SLC_TOPK)
    col = lambda a, h: a.reshape(bd, h, HEAD_DIM, 1)
    gcol = lambda a: a.reshape(bd * NSA_KV_HEADS, HEAD_DIM, 1)
    win_minor = lambda c: jnp.transpose(c[l], (0, 2, 3, 1)).reshape(bd * NSA_KV_HEADS, HEAD_DIM, WINDOW)
    ocols = _nsa_dec_attend(_nsa_dec_meta(page_table, sel2, past), page_minor(cache_slc_k), page_minor(cache_slc_v),
                            win_minor(state_win_k), win_minor(state_win_v),
                            q4.astype(F32).reshape(bd * NSA_KV_HEADS, NSA_GROUP, HEAD_DIM, 1),
                            gcol(sk_s), gcol(sv_s), gcol(wk_s), gcol(wv_s), past)
    ocols = ocols.reshape(bd, NSA_KV_HEADS, HEAD_DIM, LANES)
    to_cols = lambda a: a.reshape(bd, NSA_KV_HEADS, NSA_GROUP, HEAD_DIM).transpose(0, 2, 1, 3).reshape(bd, NSA_HEADS * HEAD_DIM)
    oc = o_cmp.reshape(bd, NSA_HEADS * HEAD_DIM)
    osl = to_cols(ocols[..., 0:2 * NSA_GROUP:2].transpose(0, 1, 3, 2))
    ow = to_cols(ocols[..., 1:2 * NSA_GROUP:2].transpose(0, 1, 3, 2))
    gates = gl_s[:, :GL_LOGF].reshape(bd, NSA_KV_HEADS, NSA_GROUP, N_BRANCH)
    gexp = lambda br: to_cols(jnp.broadcast_to(gates[..., br:br + 1], (bd, NSA_KV_HEADS, NSA_GROUP, HEAD_DIM)))

    fox_minor = lambda c: jnp.transpose(c, (0, 1, 3, 4, 2))
    of_cols = _fox_decode(page_table, fox_minor(cache_fox_k), fox_minor(cache_fox_v),
                          jnp.transpose(cache_fox_logf, (0, 1, 3, 2)),
                          col(qf_s.astype(F32), FOX_HEADS), col(fk_s, FOX_HEADS), col(fv_s, FOX_HEADS),
                          gl_s[:, GL_LOGF:GL_LOGF + FOX_HEADS].reshape(bd, FOX_HEADS, 1))
    of_s = of_cols[:, :, :FOX_HEADS].transpose(0, 2, 1).reshape(bd, FOX_HEADS * HEAD_DIM)
    x1_s = _outproj_dec(oc, osl, ow, gexp(0), gexp(1), gexp(2), of_s, xs2, g0, b0, w_on, w_of, g1, b1)
    y_s = _moe_ln2(x1_s, wr_t, br_col, w_exp_gate[l], w_exp_up[l], w_exp_down[l], wsg, wsu, wsd, g2, b2, 128, 128)

    ss5 = lambda a, h: a.reshape(1, bd, 1, h, HEAD_DIM)
    keep_s = min(WINDOW, past + 1)
    win_new = lambda buf, new: jnp.concatenate([buf, ss5(new, NSA_KV_HEADS)], axis=2)[:, :, buf.shape[2] + 1 - keep_s:]
    s_outs = (ss5(fk_s, FOX_HEADS), ss5(fv_s, FOX_HEADS),
              gl_s[:, GL_LOGF:GL_LOGF + FOX_HEADS].reshape(1, bd, 1, FOX_HEADS),
              ss5(ck_s, NSA_KV_HEADS), ss5(cv_s, NSA_KV_HEADS), ss5(sk_s, NSA_KV_HEADS), ss5(sv_s, NSA_KV_HEADS),
              win_new(state_win_k, wk_s), win_new(state_win_v, wv_s))
    return (y_p.reshape(batch, seq, D_MODEL), y_s.reshape(bd, 1, D_MODEL)) + p_outs + s_outs
```
